```python
import math, functools
import jax, jax.numpy as jnp
from jax import lax
import numpy as np

D_MODEL = 1024
BATCH = 4
SEQ = 8192
DEPTH = 1
DEC_BATCH = 128
DEC_SEQ = 8
PAST_LEN = 8192
PAGE_SIZE = 128

SSM_GROUP_CH = 16
SSM_GROUPS = D_MODEL // 32
SSM_WIDTH = SSM_GROUPS * SSM_GROUP_CH
SSM_STATE = 64
DT_MIN = 0.001
DT_MAX = 0.1
N_HEADS = 8
HEAD_DIM = 64
ATTN_WIDTH = N_HEADS * HEAD_DIM
IDX_HEADS = 4
IDX_DIM = 64
TOP_K = 256
Q_BLOCK = 128
D_FF = 4 * D_MODEL
EPS = 1e-6
IN_WIDTHS = (SSM_WIDTH, ATTN_WIDTH, ATTN_WIDTH, ATTN_WIDTH, IDX_HEADS * IDX_DIM, IDX_DIM, IDX_HEADS, D_MODEL, D_MODEL)
D_IN = sum(IN_WIDTHS)

kernel_name = 'hybrid_s5_dsa_gated_decoder_step'

F32 = jnp.float32


def rmsnorm(x, g):
    xf = x.astype(F32)
    y = xf * lax.rsqrt(jnp.mean(xf * xf, axis=-1, keepdims=True) + EPS)
    return (y * g.astype(F32)).astype(x.dtype)


def split_cols(p):
    out = []
    start = 0
    for w in IN_WIDTHS:
        out.append(p[..., start:start + w])
        start += w
    return out


def gather_rows(rows, idx):
    return jax.vmap(lambda r, i: r[i])(rows, idx)


def s5_branch(u, s0_re, s0_im, lam_re, lam_im, log_step, b_re, b_im, c_re, c_im, d_skip):
    u = u.astype(F32)
    lam_re = lam_re.astype(F32)
    lam_im = lam_im.astype(F32)
    b_re = b_re.astype(F32)
    b_im = b_im.astype(F32)
    dt = jnp.exp(log_step.astype(F32))[:, None]
    mag = jnp.exp(lam_re * dt)
    ang = lam_im * dt
    ab_re = mag * jnp.cos(ang)
    ab_im = mag * jnp.sin(ang)
    nr = ab_re - 1.0
    ni = ab_im
    den = lam_re * lam_re + lam_im * lam_im
    z_re = (nr * lam_re + ni * lam_im) / den
    z_im = (ni * lam_re - nr * lam_im) / den
    bb_re = z_re[..., None] * b_re - z_im[..., None] * b_im
    bb_im = z_re[..., None] * b_im + z_im[..., None] * b_re
    bu_re = jnp.einsum('btgh,gph->btgp', u, bb_re)
    bu_im = jnp.einsum('btgh,gph->btgp', u, bb_im)
    s0_re = s0_re.astype(F32)
    s0_im = s0_im.astype(F32)
    bu_re = bu_re.at[:, 0].add(ab_re * s0_re - ab_im * s0_im)
    bu_im = bu_im.at[:, 0].add(ab_re * s0_im + ab_im * s0_re)
    a_re = jnp.broadcast_to(ab_re, bu_re.shape)
    a_im = jnp.broadcast_to(ab_im, bu_re.shape)

    def combine(e1, e2):
        ar1, ai1, br1, bi1 = e1
        ar2, ai2, br2, bi2 = e2
        return (ar1 * ar2 - ai1 * ai2, ar1 * ai2 + ai1 * ar2,
                ar2 * br1 - ai2 * bi1 + br2, ar2 * bi1 + ai2 * br1 + bi2)

    _, _, x_re, x_im = lax.associative_scan(combine, (a_re, a_im, bu_re, bu_im), axis=1)
    y = (jnp.einsum('btgp,ghp->btgh', x_re, c_re.astype(F32))
         - jnp.einsum('btgp,ghp->btgh', x_im, c_im.astype(F32))
         + d_skip.astype(F32) * u)
    return y, x_re[:, -1], x_im[:, -1]


def index_select(qi, wi, kidx, q_pos, n_top):
    s = jnp.einsum('bqhd,bsd->bqhs', qi, kidx, preferred_element_type=F32) * (IDX_DIM ** -0.5)
    score = jnp.einsum('bqhs,bqh->bqs', jax.nn.relu(s), wi.astype(F32)) * (IDX_HEADS ** -0.5)
    key_pos = jnp.arange(kidx.shape[1], dtype=jnp.int32)
    allowed = key_pos[None, :] <= q_pos[:, None]
    score = jnp.where(allowed[None], score, -jnp.inf)
    _, idx = lax.top_k(score, n_top)
    valid = idx <= q_pos[None, :, None]
    return idx, valid


def attend(q, k_sel, v_sel, valid):
    logits = jnp.einsum('bqhd,bqkhd->bqhk', q, k_sel, preferred_element_type=F32) * (HEAD_DIM ** -0.5)
    logits = jnp.where(valid[:, :, None, :], logits, -jnp.inf)
    p = jax.nn.softmax(logits, axis=-1)
    return jnp.einsum('bqhk,bqkhd->bqhd', p.astype(v_sel.dtype), v_sel)


def prompt_attention(q, k, v, qi, ki, wi):
    B, S = q.shape[:2]
    n_top = min(TOP_K, S // 4)
    nb = S // Q_BLOCK

    def blockify(a):
        return jnp.moveaxis(a.reshape((B, nb, Q_BLOCK) + a.shape[2:]), 1, 0)

    def one_block(args):
        qb, qib, wib, pos = args
        idx, valid = index_select(qib, wib, ki, pos, n_top)
        return attend(qb, gather_rows(k, idx), gather_rows(v, idx), valid)

    pos = jnp.arange(S, dtype=jnp.int32).reshape(nb, Q_BLOCK)
    out = lax.map(one_block, (blockify(q), blockify(qi), blockify(wi), pos))
    return jnp.moveaxis(out, 0, 1).reshape(B, S, ATTN_WIDTH)


def sample_attention(q, k, v, qi, ki, wi, cache_k, cache_v, cache_kidx, page_table, layer_idx):
    Bd, T = q.shape[:2]
    n_pages = page_table.shape[1]
    past_len = n_pages * PAGE_SIZE
    kidx_past = cache_kidx[layer_idx, page_table].reshape(Bd, past_len, IDX_DIM)
    keys = jnp.concatenate([kidx_past.astype(ki.dtype), ki], axis=1)
    q_pos = past_len + jnp.arange(T, dtype=jnp.int32)
    n_top = min(TOP_K, (past_len + T) // 4)
    idx, valid = index_select(qi, wi, keys, q_pos, n_top)
    is_new = (idx >= past_len)[..., None, None]
    pidx = jnp.minimum(idx, past_len - 1)
    phys = jnp.take_along_axis(page_table, (pidx // PAGE_SIZE).reshape(Bd, -1), axis=1).reshape(idx.shape)
    off = pidx % PAGE_SIZE
    nidx = jnp.clip(idx - past_len, 0, T - 1)
    k_sel = jnp.where(is_new, gather_rows(k, nidx), cache_k[layer_idx, phys, off].astype(k.dtype))
    v_sel = jnp.where(is_new, gather_rows(v, nidx), cache_v[layer_idx, phys, off].astype(v.dtype))
    return attend(q, k_sel, v_sel, valid)


def layer(x, s0_re, s0_im, attn_fn, lp):
    (norm_mix, w_in, lam_re, lam_im, log_step, b_re, b_im, c_re, c_im, d_skip, w_glu, b_glu,
     w_branch_ssm, w_branch_attn, w_out, norm_mlp, w_up, w_down) = lp
    B, T, _ = x.shape
    xn = rmsnorm(x, norm_mix)
    u, q, k, v, qi, ki, wi, ga, gb = split_cols(xn @ w_in)
    y_ssm, s_re, s_im = s5_branch(u.reshape(B, T, SSM_GROUPS, SSM_GROUP_CH), s0_re, s0_im,
                                  lam_re, lam_im, log_step, b_re, b_im, c_re, c_im, d_skip)
    y_ssm = jax.nn.gelu(y_ssm.reshape(B, T, SSM_WIDTH))
    y_ssm = (y_ssm * jax.nn.sigmoid(y_ssm @ w_glu.astype(F32) + b_glu.astype(F32))).astype(x.dtype)
    q = q.reshape(B, T, N_HEADS, HEAD_DIM)
    k = k.reshape(B, T, N_HEADS, HEAD_DIM)
    v = v.reshape(B, T, N_HEADS, HEAD_DIM)
    qi = qi.reshape(B, T, IDX_HEADS, IDX_DIM)
    y_att = attn_fn(q, k, v, qi, ki, wi).reshape(B, T, ATTN_WIDTH).astype(x.dtype)
    merged = jax.nn.sigmoid(ga) * (y_ssm @ w_branch_ssm) + jax.nn.sigmoid(gb) * (y_att @ w_branch_attn)
    h = x + merged @ w_out
    hn = rmsnorm(h, norm_mlp)
    h = h + jnp.square(jax.nn.relu(hn @ w_up)) @ w_down
    return h, k, v, ki, s_re.astype(x.dtype), s_im.astype(x.dtype)


def setup_inputs(seed: int = 0) -> dict:
    key = jax.random.key(seed)
    ks = jax.random.split(key, 32)
    n_pages = PAST_LEN // PAGE_SIZE
    n_used = DEC_BATCH * n_pages
    n_phys = (n_used * 5) // 4
    nrm = jax.random.normal
    G, P, Hg = SSM_GROUPS, SSM_STATE, SSM_GROUP_CH
    page_table = jax.random.permutation(ks[8], n_phys)[:n_used].reshape(DEC_BATCH, n_pages).astype(jnp.int32)
    lam_re = -0.5 + 0.01 * nrm(ks[10], (DEPTH, G, P), F32)
    lam_im = math.pi * jnp.broadcast_to(jnp.arange(P, dtype=F32), (DEPTH, G, P)) + 0.01 * nrm(ks[11], (DEPTH, G, P), F32)
    log_step = jax.random.uniform(ks[12], (DEPTH, G), F32, math.log(DT_MIN), math.log(DT_MAX))
    return {
        'x_prompt': nrm(ks[0], (BATCH, SEQ, D_MODEL), F32),
        'x_sample': nrm(ks[1], (DEC_BATCH, DEC_SEQ, D_MODEL), F32),
        'cache_k': nrm(ks[2], (DEPTH, n_phys, PAGE_SIZE, N_HEADS, HEAD_DIM), F32),
        'cache_v': nrm(ks[3], (DEPTH, n_phys, PAGE_SIZE, N_HEADS, HEAD_DIM), F32),
        'cache_kidx': nrm(ks[4], (DEPTH, n_phys, PAGE_SIZE, IDX_DIM), F32),
        'state_ssm_re': 0.5 * nrm(ks[5], (DEPTH, DEC_BATCH, G, P), F32),
        'state_ssm_im': 0.5 * nrm(ks[6], (DEPTH, DEC_BATCH, G, P), F32),
        'page_table': page_table,
        'norm_mix': 1.0 + 0.02 * nrm(ks[9], (DEPTH, D_MODEL), F32),
        'w_in': nrm(ks[13], (DEPTH, D_MODEL, D_IN), F32) * D_MODEL ** -0.5,
        'lam_re': lam_re,
        'lam_im': lam_im,
        'log_step': log_step,
        'b_re': nrm(ks[14], (DEPTH, G, P, Hg), F32) * (2 * Hg) ** -0.5,
        'b_im': nrm(ks[15], (DEPTH, G, P, Hg), F32) * (2 * Hg) ** -0.5,
        'c_re': nrm(ks[16], (DEPTH, G, Hg, P), F32) * (2 * P) ** -0.5,
        'c_im': nrm(ks[17], (DEPTH, G, Hg, P), F32) * (2 * P) ** -0.5,
        'd_skip': nrm(ks[18], (DEPTH, G, Hg), F32),
        'w_glu': nrm(ks[19], (DEPTH, SSM_WIDTH, SSM_WIDTH), F32) * SSM_WIDTH ** -0.5,
        'b_glu': 0.02 * nrm(ks[20], (DEPTH, SSM_WIDTH), F32),
        'w_branch_ssm': nrm(ks[21], (DEPTH, SSM_WIDTH, D_MODEL), F32) * SSM_WIDTH ** -0.5,
        'w_branch_attn': nrm(ks[22], (DEPTH, ATTN_WIDTH, D_MODEL), F32) * ATTN_WIDTH ** -0.5,
        'w_out': nrm(ks[23], (DEPTH, D_MODEL, D_MODEL), F32) * D_MODEL ** -0.5,
        'norm_mlp': 1.0 + 0.02 * nrm(ks[24], (DEPTH, D_MODEL), F32),
        'w_up': nrm(ks[25], (DEPTH, D_MODEL, D_FF), F32) * D_MODEL ** -0.5,
        'w_down': nrm(ks[26], (DEPTH, D_FF, D_MODEL), F32) * D_FF ** -0.5,
        'norm_final': 1.0 + 0.02 * nrm(ks[27], (D_MODEL,), F32),
    }


def reference(x_prompt, x_sample, cache_k, cache_v, cache_kidx, state_ssm_re, state_ssm_im, page_table,
              norm_mix, w_in, lam_re, lam_im, log_step, b_re, b_im, c_re, c_im, d_skip, w_glu, b_glu,
              w_branch_ssm, w_branch_attn, w_out, norm_mlp, w_up, w_down, norm_final):
    hp = x_prompt
    hs = x_sample
    kp, vp, kip, srp, sip = [], [], [], [], []
    ksm, vsm, kism, srs, sis = [], [], [], [], []
    Bp = x_prompt.shape[0]
    for l in range(DEPTH):
        lp = (norm_mix[l], w_in[l], lam_re[l], lam_im[l], log_step[l], b_re[l], b_im[l], c_re[l], c_im[l],
              d_skip[l], w_glu[l], b_glu[l], w_branch_ssm[l], w_branch_attn[l], w_out[l], norm_mlp[l],
              w_up[l], w_down[l])
        zero_state = jnp.zeros((Bp, SSM_GROUPS, SSM_STATE), F32)
        hp, k1, v1, ki1, sr1, si1 = layer(hp, zero_state, zero_state, prompt_attention, lp)
        s_attn = functools.partial(sample_attention, cache_k=cache_k, cache_v=cache_v,
                                   cache_kidx=cache_kidx, page_table=page_table, layer_idx=l)
        hs, k2, v2, ki2, sr2, si2 = layer(hs, state_ssm_re[l], state_ssm_im[l], s_attn, lp)
        kp.append(k1); vp.append(v1); kip.append(ki1); srp.append(sr1); sip.append(si1)
        ksm.append(k2); vsm.append(v2); kism.append(ki2); srs.append(sr2); sis.append(si2)
    y_prompt = rmsnorm(hp, norm_final)
    y_sample = rmsnorm(hs, norm_final)
    return (y_prompt, y_sample,
            jnp.stack(kp), jnp.stack(vp), jnp.stack(kip), jnp.stack(srp), jnp.stack(sip),
            jnp.stack(ksm), jnp.stack(vsm), jnp.stack(kism), jnp.stack(srs), jnp.stack(sis))
```

```python
import functools

import jax
import jax.numpy as jnp
from jax import lax
from jax.experimental import pallas as pl
from jax.experimental.pallas import tpu as pltpu

F32 = jnp.float32
BF16 = jnp.bfloat16
I32 = jnp.int32

EPS = 1e-6
TOP_K = 256
IDX_HEADS = 4
INT_MIN = -(2 ** 31)
NEG = -1e30
LANE = 128
SUBLANE = 8
VMEM_LIMIT = 56 * 1024 * 1024


def _const_spec(shape):
    nd = len(shape)
    return pl.BlockSpec(shape, lambda *_: (0,) * nd, pipeline_mode=pl.Buffered(1))


def _rms(x, g):
    return x * lax.rsqrt(jnp.mean(x * x, axis=-1, keepdims=True) + EPS) * g


def _inproj_body(x_ref, g_ref, w_ref, u_ref, q_ref, k_ref, kb_ref, v_ref, vb_ref, qi_ref, kiw_ref,
                 *, sw, aw, iw, q_scale, qi_scale):
    xn = _rms(x_ref[...], g_ref[...]).astype(BF16)
    p = jnp.dot(xn, w_ref[...], preferred_element_type=F32)
    o = 0
    u_ref[...] = p[:, o:o + sw]
    o += sw
    q_ref[...] = (p[:, o:o + aw] * q_scale).astype(BF16)
    o += aw
    k = p[:, o:o + aw]
    k_ref[...] = k
    kb_ref[...] = k.astype(BF16)
    o += aw
    v = p[:, o:o + aw]
    v_ref[...] = v
    vb_ref[...] = v.astype(BF16)
    o += aw
    qi_ref[...] = (p[:, o:o + iw] * qi_scale).astype(BF16)
    o += iw
    kiw_ref[...] = p[:, o:o + LANE]


def _inproj(x2d, g, w_main, *, sw, aw, iw, q_scale, qi_scale, tm):
    n, d = x2d.shape
    wtot = w_main.shape[1]
    row = lambda width: pl.BlockSpec((tm, width), lambda i: (i, 0))
    return pl.pallas_call(
        functools.partial(_inproj_body, sw=sw, aw=aw, iw=iw, q_scale=q_scale, qi_scale=qi_scale),
        grid=(n // tm,),
        in_specs=[row(d), _const_spec((1, d)), _const_spec((d, wtot))],
        out_specs=[row(sw), row(aw), row(aw), row(aw), row(aw), row(aw), row(iw), row(LANE)],
        out_shape=[
            jax.ShapeDtypeStruct((n, sw), F32),
            jax.ShapeDtypeStruct((n, aw), BF16),
            jax.ShapeDtypeStruct((n, aw), F32),
            jax.ShapeDtypeStruct((n, aw), BF16),
            jax.ShapeDtypeStruct((n, aw), F32),
            jax.ShapeDtypeStruct((n, aw), BF16),
            jax.ShapeDtypeStruct((n, iw), BF16),
            jax.ShapeDtypeStruct((n, LANE), F32),
        ],
        compiler_params=pltpu.CompilerParams(dimension_semantics=("arbitrary",), vmem_limit_bytes=VMEM_LIMIT),
        name="inproj",
    )(x2d, g, w_main)


def _s5_param_body(lre_ref, lim_ref, ls_ref, bre_ref, bim_ref, are_ref, aim_ref, bbre_ref, bbim_ref):
    lam_re = lre_ref[...]
    lam_im = lim_ref[...]
    dt = jnp.exp(ls_ref[...])
    mag = jnp.exp(lam_re * dt)
    ang = lam_im * dt
    ab_re = mag * jnp.cos(ang)
    ab_im = mag * jnp.sin(ang)
    nr = ab_re - 1.0
    ni = ab_im
    den = lam_re * lam_re + lam_im * lam_im
    z_re = (nr * lam_re + ni * lam_im) / den
    z_im = (ni * lam_re - nr * lam_im) / den
    are_ref[...] = ab_re
    aim_ref[...] = ab_im
    b_re = bre_ref[...]
    b_im = bim_ref[...]
    bbre_ref[...] = z_re[:, None, :] * b_re - z_im[:, None, :] * b_im
    bbim_ref[...] = z_re[:, None, :] * b_im + z_im[:, None, :] * b_re


def _s5_params(lam_re, lam_im, log_step, b_re, b_im):
    g, p = lam_re.shape
    hg = b_re.shape[2]
    bt_re = jnp.swapaxes(b_re, 1, 2)
    bt_im = jnp.swapaxes(b_im, 1, 2)
    return pl.pallas_call(
        _s5_param_body,
        out_shape=[jax.ShapeDtypeStruct((g, p), F32), jax.ShapeDtypeStruct((g, p), F32),
                   jax.ShapeDtypeStruct((g, hg, p), F32), jax.ShapeDtypeStruct((g, hg, p), F32)],
        name="s5_params",
    )(lam_re, lam_im, log_step.reshape(g, 1), bt_re, bt_im)


def _s5_body(u_ref, s0re_ref, s0im_ref, are_ref, aim_ref, bre_ref, bim_ref, cre_ref, cimn_ref, d_ref,
             wglu_ref, bglu_ref, y_ref, sre_ref, sim_ref, xre, xim, car_re, car_im, *, nb, kt_n, lane_chunk):
    step = pl.program_id(0)
    rows, sw = u_ref.shape
    xw = xre.shape[1]
    kw = sw // kt_n
    xk = xw // kt_n

    @pl.when(step == 0)
    def _():
        car_re[...] = s0re_ref[...]
        car_im[...] = s0im_ref[...]

    u = u_ref[...]
    ub = u.astype(BF16)
    for kt in range(kt_n):
        xre[:, kt * xk:(kt + 1) * xk] = jnp.dot(ub[:, kt * kw:(kt + 1) * kw], bre_ref[kt],
                                                preferred_element_type=F32)
        xim[:, kt * xk:(kt + 1) * xk] = jnp.dot(ub[:, kt * kw:(kt + 1) * kw], bim_ref[kt],
                                                preferred_element_type=F32)

    if nb < SUBLANE:
        assert 2 * nb == SUBLANE
        lo = lax.broadcasted_iota(I32, (SUBLANE, lane_chunk), 0) < nb
        for c in range(xw // lane_chunk):
            cs = slice(c * lane_chunk, (c + 1) * lane_chunk)
            a_r = jnp.broadcast_to(are_ref[:, cs], (SUBLANE, lane_chunk))
            a_i = jnp.broadcast_to(aim_ref[:, cs], (SUBLANE, lane_chunk))

            def tile_step(j, carry, cs=cs, a_r=a_r, a_i=a_i):
                cr, ci = carry
                r0 = pl.multiple_of(j * SUBLANE, SUBLANE)
                tr = xre[pl.ds(r0, SUBLANE), cs]
                ti = xim[pl.ds(r0, SUBLANE), cs]
                x1r = a_r * cr - a_i * ci + tr
                x1i = a_r * ci + a_i * cr + ti
                sr = pltpu.roll(x1r, nb, 0)
                si = pltpu.roll(x1i, nb, 0)
                x2r = a_r * sr - a_i * si + tr
                x2i = a_r * si + a_i * sr + ti
                xre[pl.ds(r0, SUBLANE), cs] = jnp.where(lo, x1r, x2r)
                xim[pl.ds(r0, SUBLANE), cs] = jnp.where(lo, x1i, x2i)
                return pltpu.roll(x2r, nb, 0), pltpu.roll(x2i, nb, 0)

            cr, ci = lax.fori_loop(0, rows // SUBLANE, tile_step, (car_re[:, cs], car_im[:, cs]))
            car_re[:, cs] = cr
            car_im[:, cs] = ci
    else:
        a_r = are_ref[...]
        a_i = aim_ref[...]
        for t in range(rows // nb):
            rs = slice(t * nb, (t + 1) * nb)
            if t == 0:
                pr, pi = car_re[...], car_im[...]
            else:
                ps = slice((t - 1) * nb, t * nb)
                pr, pi = xre[ps, :], xim[ps, :]
            xre[rs, :] = a_r * pr - a_i * pi + xre[rs, :]
            xim[rs, :] = a_r * pi + a_i * pr + xim[rs, :]
        ls = slice(rows - nb, rows)
        car_re[...] = xre[ls, :]
        car_im[...] = xim[ls, :]

    sre_ref[...] = car_re[...]
    sim_ref[...] = car_im[...]

    ys = []
    for kt in range(kt_n):
        xs = slice(kt * xk, (kt + 1) * xk)
        ys.append(jnp.dot(xre[:, xs].astype(BF16), cre_ref[kt], preferred_element_type=F32)
                  + jnp.dot(xim[:, xs].astype(BF16), cimn_ref[kt], preferred_element_type=F32))
    y = jnp.concatenate(ys, axis=1) + d_ref[...] * u
    y = jax.nn.gelu(y)
    gate = jnp.dot(y.astype(BF16), wglu_ref[...], preferred_element_type=F32) + bglu_ref[...]
    y_ref[...] = (y * jax.nn.sigmoid(gate)).astype(BF16)


def _s5_branch(u_tm, s0_re, s0_im, nb, sp, rows_per_step):
    n, sw = u_tm.shape
    nbp, xw = s0_re.shape
    kt_n = sp["bre"].shape[0]
    r = rows_per_step
    body = functools.partial(_s5_body, nb=nb, kt_n=kt_n, lane_chunk=4 * LANE)
    return pl.pallas_call(
        body,
        grid=(n // r,),
        in_specs=[pl.BlockSpec((r, sw), lambda i: (i, 0)),
                  _const_spec((nbp, xw)), _const_spec((nbp, xw)),
                  _const_spec((1, xw)), _const_spec((1, xw)),
                  _const_spec(sp["bre"].shape), _const_spec(sp["bim"].shape),
                  _const_spec(sp["cre"].shape), _const_spec(sp["cimn"].shape),
                  _const_spec((1, sw)), _const_spec((sw, sw)), _const_spec((1, sw))],
        out_specs=[pl.BlockSpec((r, sw), lambda i: (i, 0)),
                   pl.BlockSpec((nbp, xw), lambda i: (0, 0)), pl.BlockSpec((nbp, xw), lambda i: (0, 0))],
        out_shape=[jax.ShapeDtypeStruct((n, sw), BF16),
                   jax.ShapeDtypeStruct((nbp, xw), F32), jax.ShapeDtypeStruct((nbp, xw), F32)],
        scratch_shapes=[pltpu.VMEM((r, xw), F32), pltpu.VMEM((r, xw), F32),
                        pltpu.VMEM((nbp, xw), F32), pltpu.VMEM((nbp, xw), F32)],
        compiler_params=pltpu.CompilerParams(dimension_semantics=("arbitrary",), vmem_limit_bytes=VMEM_LIMIT),
        name="s5_branch",
    )(u_tm, s0_re, s0_im, sp["a_re"], sp["a_im"], sp["bre"], sp["bim"], sp["cre"], sp["cimn"],
      sp["d"], sp["wglu"], sp["bglu"])


def _s5_setup(lam_re, lam_im, log_step, b_re, b_im, c_re, c_im, d_skip, w_glu, b_glu, kt_n=2):
    g, p = lam_re.shape
    hg = b_re.shape[2]
    ab_re, ab_im, bbt_re, bbt_im = _s5_params(lam_re, lam_im, log_step, b_re, b_im)
    gl = g // kt_n
    eye = jnp.eye(gl, dtype=F32)

    def bmat(bbt):
        return jnp.einsum("kghp,gf->kghfp", bbt.reshape(kt_n, gl, hg, p), eye).reshape(
            kt_n, gl * hg, gl * p).astype(BF16)

    def cmat(c):
        return jnp.einsum("kghp,gf->kfpgh", c.reshape(kt_n, gl, hg, p), eye).reshape(
            kt_n, gl * p, gl * hg).astype(BF16)

    return dict(a_re=ab_re.reshape(1, g * p), a_im=ab_im.reshape(1, g * p),
                bre=bmat(bbt_re), bim=bmat(bbt_im), cre=cmat(c_re), cimn=cmat(-c_im),
                d=d_skip.reshape(1, g * hg), wglu=w_glu.astype(BF16), bglu=b_glu.reshape(1, -1))


def _sortable_key(score):
    bits = lax.bitcast_convert_type(score, I32)
    return bits ^ ((bits >> 31) & 0x7FFFFFFF)


def _prompt_attn_body(qit_ref, wit_ref, qpt_ref, ki_ref, k_ref, vt_ref, lt_ref, o_ref,
                      keys_s, m_s, l_s, acc_s, *, n_top, n_heads, head_dim):
    i = pl.program_id(1)
    tk = ki_ref.shape[2]
    idim = ki_ref.shape[3]
    tq = qit_ref.shape[2]
    nkb = i + 1
    sub_iota = lax.broadcasted_iota(I32, (tk, tq), 0)
    q_pos = i * tq + lax.broadcasted_iota(I32, (tk, tq), 1)

    def score_block(j, _):
        kib = ki_ref[0, j]
        sc = None
        for h in range(IDX_HEADS):
            s = jnp.dot(kib, qit_ref[0, h * idim:(h + 1) * idim, :], preferred_element_type=F32)
            r = jnp.maximum(s, 0.0) * wit_ref[0, h:h + 1, :]
            sc = r if sc is None else sc + r
        sc = sc * (IDX_HEADS ** -0.5)
        key = jnp.where(sub_iota + j * tk <= q_pos, _sortable_key(sc), INT_MIN)
        keys_s[j] = key
        return 0

    lax.fori_loop(0, nkb, score_block, 0)

    def count(pred):
        def body(j, acc):
            hit = jnp.where(pred(keys_s[j]), 1, 0)
            return acc + hit.reshape(tk // SUBLANE, SUBLANE, tq).sum(axis=0)
        acc = lax.fori_loop(0, nkb, body, jnp.zeros((SUBLANE, tq), I32))
        return acc.sum(axis=0, keepdims=True)

    thr = jnp.where(count(lambda kb: kb >= 0) >= n_top, 0, INT_MIN).astype(I32)

    def bit_step(b, thr):
        cand = thr | lax.shift_left(jnp.int32(1), 30 - b)
        return jnp.where(count(lambda kb: kb >= cand) >= n_top, cand, thr)

    thr = lax.fori_loop(0, 31, bit_step, thr)
    thr = jnp.maximum(thr, INT_MIN + 1)
    n_gt = count(lambda kb: kb > thr)
    need = (n_top - n_gt).astype(F32)

    m_s[...] = jnp.full(m_s.shape, NEG, F32)
    l_s[...] = jnp.zeros(l_s.shape, F32)
    acc_s[...] = jnp.zeros(acc_s.shape, F32)
    pair = 2 * head_dim

    def attend_block(j, eq_seen):
        kb = keys_s[j]
        eq = kb == thr
        eqf = jnp.where(eq, 1.0, 0.0)
        rank = jnp.dot(lt_ref[...], eqf.astype(BF16), preferred_element_type=F32) + eq_seen
        sel = (kb > thr) | (eq & (rank < need))
        madd = jnp.where(sel, 0.0, NEG)
        kblk = k_ref[0, j]
        vblk = vt_ref[0, j]
        for h in range(n_heads):
            c0 = (h // 2) * pair
            lg = jnp.dot(kblk[:, c0:c0 + pair], qpt_ref[0, h * pair:(h + 1) * pair, :],
                         preferred_element_type=F32) + madd
            m_old = m_s[h:h + 1, :]
            m_new = jnp.maximum(m_old, jnp.max(lg, axis=0, keepdims=True))
            alpha = jnp.exp(m_old - m_new)
            p = jnp.exp(lg - m_new)
            l_s[h:h + 1, :] = alpha * l_s[h:h + 1, :] + jnp.sum(p, axis=0, keepdims=True)
            hs = slice(h * head_dim, (h + 1) * head_dim)
            acc_s[hs, :] = alpha * acc_s[hs, :] + jnp.dot(vblk[hs, :], p.astype(BF16),
                                                          preferred_element_type=F32)
            m_s[h:h + 1, :] = m_new
        return eq_seen + jnp.sum(eqf, axis=0, keepdims=True)

    lax.fori_loop(0, nkb, attend_block, jnp.zeros((1, tq), F32))
    for h in range(n_heads):
        hs = slice(h * head_dim, (h + 1) * head_dim)
        o_ref[0, hs, :] = (acc_s[hs, :] / l_s[h:h + 1, :]).astype(o_ref.dtype)


def _prompt_attention(q, kb, vb, qi, ki, wi, *, n_heads, head_dim, blk):
    b, t, aw = q.shape
    idim = ki.shape[2]
    nblk = t // blk
    n_top = min(TOP_K, t // 4)
    pair = 2 * head_dim
    qit = jnp.swapaxes(qi, 1, 2)
    wit = jnp.pad(jnp.swapaxes(wi, 1, 2), ((0, 0), (0, SUBLANE - IDX_HEADS), (0, 0)))
    q4 = q.reshape(b, t, n_heads // 2, 2, head_dim)
    sel = jnp.eye(2, dtype=q.dtype)
    qp = jnp.einsum("btgrd,rs->btgrsd", q4, sel).reshape(b, t, n_heads * pair)
    qpt = jnp.swapaxes(qp, 1, 2)
    ki_b = ki.reshape(b, nblk, blk, idim)
    k_b = kb.reshape(b, nblk, blk, aw)
    vt_b = jnp.swapaxes(vb.reshape(b, nblk, blk, aw), 2, 3)
    lt = jnp.tril(jnp.ones((blk, blk), F32), -1).astype(BF16)

    def whole(shape):
        return pl.BlockSpec((1,) + shape, lambda bi, i: (bi, 0, 0, 0), pipeline_mode=pl.Buffered(1))

    body = functools.partial(_prompt_attn_body, n_top=n_top, n_heads=n_heads, head_dim=head_dim)
    out_t = pl.pallas_call(
        body,
        grid=(b, nblk),
        in_specs=[pl.BlockSpec((1, qit.shape[1], blk), lambda bi, i: (bi, 0, i)),
                  pl.BlockSpec((1, SUBLANE, blk), lambda bi, i: (bi, 0, i)),
                  pl.BlockSpec((1, qpt.shape[1], blk), lambda bi, i: (bi, 0, i)),
                  whole((nblk, blk, idim)), whole((nblk, blk, aw)), whole((nblk, aw, blk)),
                  _const_spec((blk, blk))],
        out_specs=pl.BlockSpec((1, aw, blk), lambda bi, i: (bi, 0, i)),
        out_shape=jax.ShapeDtypeStruct((b, aw, t), BF16),
        scratch_shapes=[pltpu.VMEM((nblk, blk, blk), I32),
                        pltpu.VMEM((SUBLANE, blk), F32), pltpu.VMEM((SUBLANE, blk), F32),
                        pltpu.VMEM((aw, blk), F32)],
        compiler_params=pltpu.CompilerParams(dimension_semantics=("arbitrary", "arbitrary"),
                                             vmem_limit_bytes=VMEM_LIMIT),
        name="prompt_attention",
    )(qit, wit, qpt, ki_b, k_b, vt_b, lt)
    return jnp.swapaxes(out_t, 1, 2)


def _sample_select_body(pt_ref, qi_ref, wi_ref, kin_ref, ut_ref, *rest, n_pages, n_top):
    page_refs = rest[:n_pages]
    madd_ref = rest[n_pages]
    keys_s = rest[n_pages + 1]
    tq = wi_ref.shape[1]
    page = page_refs[0].shape[2]
    qi = qi_ref[0]
    w = wi_ref[0]
    nt = (((1,), (1,)), ((), ()))

    def tile_keys(kpage_bf16, allowed):
        s = lax.dot_general(qi, kpage_bf16, nt, preferred_element_type=F32)
        sc = None
        for h in range(IDX_HEADS):
            r = jnp.maximum(s[h * tq:(h + 1) * tq, :], 0.0) * w[:, h:h + 1]
            sc = r if sc is None else sc + r
        key = _sortable_key(sc * (IDX_HEADS ** -0.5))
        return key if allowed is None else jnp.where(allowed, key, INT_MIN)

    for p in range(n_pages):
        keys_s[p] = tile_keys(page_refs[p][0, 0].astype(BF16), None)
    t_iota = lax.broadcasted_iota(I32, (tq, page), 0)
    lane_iota = lax.broadcasted_iota(I32, (tq, page), 1)
    keys_s[n_pages] = tile_keys(kin_ref[0], lane_iota <= t_iota)
    n_tiles = n_pages + 1

    def count(pred):
        acc = jnp.zeros((tq, page), I32)
        for p in range(n_tiles):
            acc = acc + jnp.where(pred(keys_s[p]), 1, 0)
        return jnp.sum(acc, axis=1, keepdims=True)

    thr = jnp.where(count(lambda kb: kb >= 0) >= n_top, 0, INT_MIN).astype(I32)

    def bit_step(b, thr):
        cand = thr | lax.shift_left(jnp.int32(1), 30 - b)
        return jnp.where(count(lambda kb: kb >= cand) >= n_top, cand, thr)

    thr = lax.fori_loop(0, 31, bit_step, thr)
    thr = jnp.maximum(thr, INT_MIN + 1)
    need = (n_top - count(lambda kb: kb > thr)).astype(F32)

    eq_seen = jnp.zeros((tq, 1), F32)
    for p in range(n_tiles):
        kb = keys_s[p]
        eq = kb == thr
        eqf = jnp.where(eq, 1.0, 0.0)
        rank = jnp.dot(eqf.astype(BF16), ut_ref[...], preferred_element_type=F32) + eq_seen
        sel = (kb > thr) | (eq & (rank < need))
        madd_ref[0, p] = jnp.where(sel, 0.0, NEG)
        eq_seen = eq_seen + jnp.sum(eqf, axis=1, keepdims=True)


def _sample_select(page_table, qi_ht, wi, ki_new_pad, cache_kidx, *, n_top):
    bd, n_pages = page_table.shape
    _, n_phys, page, idim = cache_kidx.shape
    tq = wi.shape[1]
    ut = jnp.triu(jnp.ones((page, page), F32), 1).astype(BF16)

    def page_spec(p):
        return pl.BlockSpec((1, 1, page, idim), lambda b, pt, p=p: (0, pt[b, p], 0, 0))

    grid_spec = pltpu.PrefetchScalarGridSpec(
        num_scalar_prefetch=1,
        grid=(bd,),
        in_specs=[pl.BlockSpec((1, IDX_HEADS * tq, idim), lambda b, pt: (b, 0, 0)),
                  pl.BlockSpec((1, tq, IDX_HEADS), lambda b, pt: (b, 0, 0)),
                  pl.BlockSpec((1, page, idim), lambda b, pt: (b, 0, 0)),
                  pl.BlockSpec((page, page), lambda b, pt: (0, 0))]
                 + [page_spec(p) for p in range(n_pages)],
        out_specs=pl.BlockSpec((1, n_pages + 1, tq, page), lambda b, pt: (b, 0, 0, 0)),
        scratch_shapes=[pltpu.VMEM((n_pages + 1, tq, page), I32)],
    )
    return pl.pallas_call(
        functools.partial(_sample_select_body, n_pages=n_pages, n_top=n_top),
        grid_spec=grid_spec,
        out_shape=jax.ShapeDtypeStruct((bd, n_pages + 1, tq, page), F32),
        compiler_params=pltpu.CompilerParams(dimension_semantics=("arbitrary",), vmem_limit_bytes=VMEM_LIMIT),
        name="sample_select",
    )(page_table, qi_ht, wi, ki_new_pad, ut, *([cache_kidx] * n_pages))


def _sample_attend_body(pt_ref, qbd_ref, madd_ref, maddn_ref, kn_ref, vn_ref, *rest, pps, n_heads, head_dim):
    k_refs = rest[:pps]
    v_refs = rest[pps:2 * pps]
    o_ref = rest[2 * pps]
    m_s, l_s, acc_s = rest[2 * pps + 1:]
    c = pl.program_id(1)
    n_chunks = pl.num_programs(1)
    tq = madd_ref.shape[2]
    qbd = qbd_ref[0]
    nt = (((1,), (1,)), ((), ()))

    @pl.when(c == 0)
    def _():
        m_s[...] = jnp.full(m_s.shape, NEG, F32)
        l_s[...] = jnp.zeros(l_s.shape, F32)
        acc_s[...] = jnp.zeros(acc_s.shape, F32)

    def update(k_tiles, v_tiles, madd_tiles):
        lgs = []
        for kt, mt in zip(k_tiles, madd_tiles):
            lg = lax.dot_general(qbd, kt, nt, preferred_element_type=F32)
            lgs.append(lg + jnp.concatenate([mt] * n_heads, axis=0))
        mx = lgs[0]
        for lg in lgs[1:]:
            mx = jnp.maximum(mx, lg)
        m_old = m_s[...]
        m_new = jnp.maximum(m_old, jnp.max(mx, axis=1, keepdims=True))
        alpha = jnp.exp(m_old - m_new)
        psum = None
        pv = None
        for lg, vt in zip(lgs, v_tiles):
            p = jnp.exp(lg - m_new)
            psum = p if psum is None else psum + p
            d = jnp.dot(p.astype(BF16), vt, preferred_element_type=F32)
            pv = d if pv is None else pv + d
        l_s[...] = alpha * l_s[...] + jnp.sum(psum, axis=1, keepdims=True)
        acc_s[...] = alpha * acc_s[...] + pv
        m_s[...] = m_new

    update([k_refs[p][0, 0].astype(BF16) for p in range(pps)],
           [v_refs[p][0, 0].astype(BF16) for p in range(pps)],
           [madd_ref[0, p] for p in range(pps)])

    @pl.when(c == n_chunks - 1)
    def _():
        update([kn_ref[0]], [vn_ref[0]], [maddn_ref[0, 0]])
        accn = acc_s[...] / l_s[...]
        lane_head = lax.broadcasted_iota(I32, (tq, n_heads * head_dim), 1) // head_dim
        out = jnp.zeros((tq, n_heads * head_dim), F32)
        for h in range(n_heads):
            out = jnp.where(lane_head == h, accn[h * tq:(h + 1) * tq, :], out)
        o_ref[0] = out.astype(o_ref.dtype)


def _sample_attend(page_table, qbd, madd, k_new_pad, v_new_pad, cache_k, cache_v, *, n_heads, head_dim, pps):
    bd, n_pages = page_table.shape
    _, n_phys, page, aw = cache_k.shape
    tq = madd.shape[2]
    rows = qbd.shape[1]

    def page_spec(p):
        return pl.BlockSpec((1, 1, page, aw), lambda b, c, pt, p=p: (0, pt[b, c * pps + p], 0, 0))

    grid_spec = pltpu.PrefetchScalarGridSpec(
        num_scalar_prefetch=1,
        grid=(bd, n_pages // pps),
        in_specs=[pl.BlockSpec((1, rows, aw), lambda b, c, pt: (b, 0, 0)),
                  pl.BlockSpec((1, pps, tq, page), lambda b, c, pt: (b, c, 0, 0)),
                  pl.BlockSpec((1, 1, tq, page), lambda b, c, pt: (b, n_pages, 0, 0)),
                  pl.BlockSpec((1, page, aw), lambda b, c, pt: (b, 0, 0)),
                  pl.BlockSpec((1, page, aw), lambda b, c, pt: (b, 0, 0))]
                 + [page_spec(p) for p in range(pps)] * 2,
        out_specs=pl.BlockSpec((1, tq, aw), lambda b, c, pt: (b, 0, 0)),
        scratch_shapes=[pltpu.VMEM((rows, 1), F32), pltpu.VMEM((rows, 1), F32), pltpu.VMEM((rows, aw), F32)],
    )
    return pl.pallas_call(
        functools.partial(_sample_attend_body, pps=pps, n_heads=n_heads, head_dim=head_dim),
        grid_spec=grid_spec,
        out_shape=jax.ShapeDtypeStruct((bd, tq, aw), BF16),
        compiler_params=pltpu.CompilerParams(dimension_semantics=("arbitrary", "arbitrary"),
                                             vmem_limit_bytes=VMEM_LIMIT),
        name="sample_attend",
    )(page_table, qbd, madd, madd, k_new_pad, v_new_pad, *([cache_k] * pps), *([cache_v] * pps))


def _tail_body(x_ref, ys_ref, ya_ref, gmix_ref, wg_ref, wbs_ref, wba_ref, wo_ref, gmlp_ref, wup_ref, wdn_ref,
               gfin_ref, o_ref):
    x = x_ref[...]
    d = x.shape[1]
    xn = _rms(x, gmix_ref[...]).astype(BF16)
    gates = jnp.dot(xn, wg_ref[...], preferred_element_type=F32)
    a = jnp.dot(ys_ref[...], wbs_ref[...], preferred_element_type=F32)
    b = jnp.dot(ya_ref[...], wba_ref[...], preferred_element_type=F32)
    merged = jax.nn.sigmoid(gates[:, :d]) * a + jax.nn.sigmoid(gates[:, d:]) * b
    h = x + jnp.dot(merged.astype(BF16), wo_ref[...], preferred_element_type=F32)
    hn = _rms(h, gmlp_ref[...]).astype(BF16)
    up = jnp.dot(hn, wup_ref[...], preferred_element_type=F32)
    act = jnp.square(jnp.maximum(up, 0.0)).astype(BF16)
    h = h + jnp.dot(act, wdn_ref[...], preferred_element_type=F32)
    o_ref[...] = _rms(h, gfin_ref[...])


def _tail(x2d, ys, ya, tw, *, tm):
    n, d = x2d.shape
    sw = ys.shape[1]
    aw = ya.shape[1]
    row = lambda width: pl.BlockSpec((tm, width), lambda i: (i, 0))
    consts = [tw["gmix"], tw["wg"], tw["wbs"], tw["wba"], tw["wo"], tw["gmlp"], tw["wup"], tw["wdn"], tw["gfin"]]
    return pl.pallas_call(
        _tail_body,
        grid=(n // tm,),
        in_specs=[row(d), row(sw), row(aw)] + [_const_spec(c.shape) for c in consts],
        out_specs=row(d),
        out_shape=jax.ShapeDtypeStruct((n, d), F32),
        compiler_params=pltpu.CompilerParams(dimension_semantics=("arbitrary",), vmem_limit_bytes=VMEM_LIMIT),
        name="tail",
    )(x2d, ys, ya, *consts)


def kernel(x_prompt, x_sample, cache_k, cache_v, cache_kidx, state_ssm_re, state_ssm_im, page_table, norm_mix, w_in,
           lam_re, lam_im, log_step, b_re, b_im, c_re, c_im, d_skip, w_glu, b_glu, w_branch_ssm, w_branch_attn,
           w_out, norm_mlp, w_up, w_down, norm_final):
    depth, d, _ = w_in.shape
    assert depth == 1, "single-layer step"
    l = 0
    bp, tp, _ = x_prompt.shape
    bd, ts, _ = x_sample.shape
    _, n_phys, page, n_heads, head_dim = cache_k.shape
    idim = cache_kidx.shape[3]
    g, p = lam_re.shape[1:]
    hg = b_re.shape[3]
    sw, aw, iw, xw = g * hg, n_heads * head_dim, IDX_HEADS * idim, g * p
    n_pages = page_table.shape[1]
    past_len = n_pages * page

    o_g = sw + 3 * aw + iw + idim + IDX_HEADS
    w_main = jnp.pad(w_in[l][:, :o_g], ((0, 0), (0, LANE - idim - IDX_HEADS))).astype(BF16)
    gmix = norm_mix[l].reshape(1, d)
    tw = dict(gmix=gmix, wg=w_in[l][:, o_g:].astype(BF16), wbs=w_branch_ssm[l].astype(BF16),
              wba=w_branch_attn[l].astype(BF16), wo=w_out[l].astype(BF16), gmlp=norm_mlp[l].reshape(1, d),
              wup=w_up[l].astype(BF16), wdn=w_down[l].astype(BF16), gfin=norm_final.reshape(1, d))
    sp = _s5_setup(lam_re[l], lam_im[l], log_step[l], b_re[l], b_im[l], c_re[l], c_im[l], d_skip[l],
                   w_glu[l], b_glu[l])
    proj = functools.partial(_inproj, g=gmix, w_main=w_main, sw=sw, aw=aw, iw=iw,
                             q_scale=head_dim ** -0.5, qi_scale=idim ** -0.5)

    def time_major(a, b, t):
        return jnp.swapaxes(a.reshape(b, t, -1), 0, 1).reshape(t * b, -1)

    def batch_major(a, b, t):
        return jnp.swapaxes(a.reshape(t, b, -1), 0, 1).reshape(b * t, -1)

    xp = x_prompt.reshape(bp * tp, d)
    u, q, k32, kb, v32, vb, qi, kiw = proj(xp, tm=512)
    ki32 = kiw[:, :idim]
    wi = kiw[:, idim:idim + IDX_HEADS]
    nbp = max(bp, SUBLANE)
    zero_state = jnp.zeros((nbp, xw), F32)
    y_tm, sre_p, sim_p = _s5_branch(time_major(u, bp, tp), zero_state, zero_state, bp, sp, rows_per_step=512)
    y_ssm = batch_major(y_tm, bp, tp)
    y_att = _prompt_attention(q.reshape(bp, tp, aw), kb.reshape(bp, tp, aw), vb.reshape(bp, tp, aw),
                              qi.reshape(bp, tp, iw), ki32.astype(BF16).reshape(bp, tp, idim),
                              wi.reshape(bp, tp, IDX_HEADS), n_heads=n_heads, head_dim=head_dim, blk=256)
    y_prompt = _tail(xp, y_ssm, y_att.reshape(bp * tp, aw), tw, tm=256).reshape(bp, tp, d)
    k_prompt = k32.reshape(1, bp, tp, n_heads, head_dim)
    v_prompt = v32.reshape(1, bp, tp, n_heads, head_dim)
    kidx_prompt = ki32.reshape(1, bp, tp, idim)
    ssm_re_prompt = sre_p[:bp].reshape(1, bp, g, p)
    ssm_im_prompt = sim_p[:bp].reshape(1, bp, g, p)

    xs = x_sample.reshape(bd * ts, d)
    u, q, k32, kb, v32, vb, qi, kiw = proj(xs, tm=512)
    ki32 = kiw[:, :idim]
    wi = kiw[:, idim:idim + IDX_HEADS]
    y_tm, sre_s, sim_s = _s5_branch(time_major(u, bd, ts), state_ssm_re[l].reshape(bd, xw),
                                    state_ssm_im[l].reshape(bd, xw), bd, sp, rows_per_step=bd * ts)
    y_ssm = batch_major(y_tm, bd, ts)
    n_top = min(TOP_K, (past_len + ts) // 4)
    qi_ht = jnp.swapaxes(qi.reshape(bd, ts, IDX_HEADS, idim), 1, 2).reshape(bd, IDX_HEADS * ts, idim)
    pad_rows = lambda a: jnp.pad(a.reshape(bd, ts, -1), ((0, 0), (0, page - ts), (0, 0)))
    madd = _sample_select(page_table, qi_ht, wi.reshape(bd, ts, IDX_HEADS), pad_rows(ki32.astype(BF16)),
                          cache_kidx[l:l + 1], n_top=n_top)
    qbd = jnp.einsum("bthd,hg->bhtgd", q.reshape(bd, ts, n_heads, head_dim),
                     jnp.eye(n_heads, dtype=BF16)).reshape(bd, n_heads * ts, aw)
    y_att = _sample_attend(page_table, qbd, madd, pad_rows(kb), pad_rows(vb),
                           cache_k[l:l + 1].reshape(1, n_phys, page, aw),
                           cache_v[l:l + 1].reshape(1, n_phys, page, aw),
                           n_heads=n_heads, head_dim=head_dim, pps=8)
    y_sample = _tail(xs, y_ssm, y_att.reshape(bd * ts, aw), tw, tm=256).reshape(bd, ts, d)
    k_sample = k32.reshape(1, bd, ts, n_heads, head_dim)
    v_sample = v32.reshape(1, bd, ts, n_heads, head_dim)
    kidx_sample = ki32.reshape(1, bd, ts, idim)
    ssm_re_sample = sre_s.reshape(1, bd, g, p)
    ssm_im_sample = sim_s.reshape(1, bd, g, p)

    return (y_prompt, y_sample, k_prompt, v_prompt, kidx_prompt, ssm_re_prompt, ssm_im_prompt,
            k_sample, v_sample, kidx_sample, ssm_re_sample, ssm_im_sample)
```

```python
import functools

import jax
import jax.numpy as jnp
from jax import lax
from jax.experimental import pallas as pl
from jax.experimental.pallas import tpu as pltpu

F32 = jnp.float32
BF16 = jnp.bfloat16
I32 = jnp.int32
I16 = jnp.int16

EPS = 1e-6
TOP_K = 256
IDX_HEADS = 4
INT_MIN = -(2 ** 31)
I16_MIN = -(2 ** 15)
LOG2E = 1.4426950408889634
NEG = -1e30
LANE = 128
SUBLANE = 8
PACK16 = 16
VMEM_LIMIT = 56 * 1024 * 1024


def _const_spec(shape):
    nd = len(shape)
    return pl.BlockSpec(shape, lambda *_: (0,) * nd, pipeline_mode=pl.Buffered(1))


def _rms(x, g):
    return x * lax.rsqrt(jnp.mean(x * x, axis=-1, keepdims=True) + EPS) * g


def _inproj_body(x_ref, g_ref, w_ref, u_ref, q_ref, k_ref, kb_ref, v_ref, vb_ref, qi_ref, kiw_ref,
                 *, sw, aw, iw, q_scale, qi_scale):
    xn = _rms(x_ref[...], g_ref[...]).astype(BF16)
    p = jnp.dot(xn, w_ref[...], preferred_element_type=F32)
    o = 0
    u_ref[...] = p[:, o:o + sw]
    o += sw
    q_ref[...] = (p[:, o:o + aw] * q_scale).astype(BF16)
    o += aw
    k = p[:, o:o + aw]
    k_ref[...] = k
    kb_ref[...] = k.astype(BF16)
    o += aw
    v = p[:, o:o + aw]
    v_ref[...] = v
    vb_ref[...] = v.astype(BF16)
    o += aw
    qi_ref[...] = (p[:, o:o + iw] * qi_scale).astype(BF16)
    o += iw
    kiw_ref[...] = p[:, o:o + LANE]


def _inproj(x2d, g, w_main, *, sw, aw, iw, q_scale, qi_scale, tm):
    n, d = x2d.shape
    wtot = w_main.shape[1]
    row = lambda width: pl.BlockSpec((tm, width), lambda i: (i, 0))
    return pl.pallas_call(
        functools.partial(_inproj_body, sw=sw, aw=aw, iw=iw, q_scale=q_scale, qi_scale=qi_scale),
        grid=(n // tm,),
        in_specs=[row(d), _const_spec((1, d)), _const_spec((d, wtot))],
        out_specs=[row(sw), row(aw), row(aw), row(aw), row(aw), row(aw), row(iw), row(LANE)],
        out_shape=[
            jax.ShapeDtypeStruct((n, sw), F32),
            jax.ShapeDtypeStruct((n, aw), BF16),
            jax.ShapeDtypeStruct((n, aw), F32),
            jax.ShapeDtypeStruct((n, aw), BF16),
            jax.ShapeDtypeStruct((n, aw), F32),
            jax.ShapeDtypeStruct((n, aw), BF16),
            jax.ShapeDtypeStruct((n, iw), BF16),
            jax.ShapeDtypeStruct((n, LANE), F32),
        ],
        compiler_params=pltpu.CompilerParams(dimension_semantics=("arbitrary",), vmem_limit_bytes=VMEM_LIMIT),
        name="inproj",
    )(x2d, g, w_main)


def _s5_param_body(lre_ref, lim_ref, ls_ref, bre_ref, bim_ref, are_ref, aim_ref, bbre_ref, bbim_ref):
    lam_re = lre_ref[...]
    lam_im = lim_ref[...]
    dt = jnp.exp(ls_ref[...])
    mag = jnp.exp(lam_re * dt)
    ang = lam_im * dt
    ab_re = mag * jnp.cos(ang)
    ab_im = mag * jnp.sin(ang)
    nr = ab_re - 1.0
    ni = ab_im
    den = lam_re * lam_re + lam_im * lam_im
    z_re = (nr * lam_re + ni * lam_im) / den
    z_im = (ni * lam_re - nr * lam_im) / den
    are_ref[...] = ab_re
    aim_ref[...] = ab_im
    b_re = bre_ref[...]
    b_im = bim_ref[...]
    bbre_ref[...] = z_re[:, None, :] * b_re - z_im[:, None, :] * b_im
    bbim_ref[...] = z_re[:, None, :] * b_im + z_im[:, None, :] * b_re


def _s5_params(lam_re, lam_im, log_step, b_re, b_im):
    g, p = lam_re.shape
    hg = b_re.shape[2]
    bt_re = jnp.swapaxes(b_re, 1, 2)
    bt_im = jnp.swapaxes(b_im, 1, 2)
    return pl.pallas_call(
        _s5_param_body,
        out_shape=[jax.ShapeDtypeStruct((g, p), F32), jax.ShapeDtypeStruct((g, p), F32),
                   jax.ShapeDtypeStruct((g, hg, p), F32), jax.ShapeDtypeStruct((g, hg, p), F32)],
        name="s5_params",
    )(lam_re, lam_im, log_step.reshape(g, 1), bt_re, bt_im)


def _s5_body(u_ref, s0re_ref, s0im_ref, are_ref, aim_ref, bre_ref, bim_ref, cre_ref, cimn_ref, d_ref,
             wglu_ref, bglu_ref, y_ref, sre_ref, sim_ref, xre, xim, car_re, car_im, *, nb, kt_n, lane_chunk):
    step = pl.program_id(0)
    rows, sw = u_ref.shape
    xw = xre.shape[1]
    kw = sw // kt_n
    xk = xw // kt_n

    @pl.when(step == 0)
    def _():
        car_re[...] = s0re_ref[...]
        car_im[...] = s0im_ref[...]

    u = u_ref[...]
    ub = u.astype(BF16)
    for kt in range(kt_n):
        xre[:, kt * xk:(kt + 1) * xk] = jnp.dot(ub[:, kt * kw:(kt + 1) * kw], bre_ref[kt],
                                                preferred_element_type=F32)
        xim[:, kt * xk:(kt + 1) * xk] = jnp.dot(ub[:, kt * kw:(kt + 1) * kw], bim_ref[kt],
                                                preferred_element_type=F32)

    if nb < SUBLANE:
        assert 2 * nb == SUBLANE
        lo = lax.broadcasted_iota(I32, (SUBLANE, lane_chunk), 0) < nb
        for c in range(xw // lane_chunk):
            cs = slice(c * lane_chunk, (c + 1) * lane_chunk)
            a_r = jnp.broadcast_to(are_ref[:, cs], (SUBLANE, lane_chunk))
            a_i = jnp.broadcast_to(aim_ref[:, cs], (SUBLANE, lane_chunk))

            def tile_step(j, carry, cs=cs, a_r=a_r, a_i=a_i):
                cr, ci = carry
                r0 = pl.multiple_of(j * SUBLANE, SUBLANE)
                tr = xre[pl.ds(r0, SUBLANE), cs]
                ti = xim[pl.ds(r0, SUBLANE), cs]
                x1r = a_r * cr - a_i * ci + tr
                x1i = a_r * ci + a_i * cr + ti
                sr = pltpu.roll(x1r, nb, 0)
                si = pltpu.roll(x1i, nb, 0)
                x2r = a_r * sr - a_i * si + tr
                x2i = a_r * si + a_i * sr + ti
                xre[pl.ds(r0, SUBLANE), cs] = jnp.where(lo, x1r, x2r)
                xim[pl.ds(r0, SUBLANE), cs] = jnp.where(lo, x1i, x2i)
                return pltpu.roll(x2r, nb, 0), pltpu.roll(x2i, nb, 0)

            cr, ci = lax.fori_loop(0, rows // SUBLANE, tile_step, (car_re[:, cs], car_im[:, cs]))
            car_re[:, cs] = cr
            car_im[:, cs] = ci
    else:
        a_r = are_ref[...]
        a_i = aim_ref[...]
        for t in range(rows // nb):
            rs = slice(t * nb, (t + 1) * nb)
            if t == 0:
                pr, pi = car_re[...], car_im[...]
            else:
                ps = slice((t - 1) * nb, t * nb)
                pr, pi = xre[ps, :], xim[ps, :]
            xre[rs, :] = a_r * pr - a_i * pi + xre[rs, :]
            xim[rs, :] = a_r * pi + a_i * pr + xim[rs, :]
        ls = slice(rows - nb, rows)
        car_re[...] = xre[ls, :]
        car_im[...] = xim[ls, :]

    sre_ref[...] = car_re[...]
    sim_ref[...] = car_im[...]

    ys = []
    for kt in range(kt_n):
        xs = slice(kt * xk, (kt + 1) * xk)
        ys.append(jnp.dot(xre[:, xs].astype(BF16), cre_ref[kt], preferred_element_type=F32)
                  + jnp.dot(xim[:, xs].astype(BF16), cimn_ref[kt], preferred_element_type=F32))
    y = jnp.concatenate(ys, axis=1) + d_ref[...] * u
    y = jax.nn.gelu(y)
    gate = jnp.dot(y.astype(BF16), wglu_ref[...], preferred_element_type=F32) + bglu_ref[...]
    y_ref[...] = (y * jax.nn.sigmoid(gate)).astype(BF16)


def _s5_branch(u_tm, s0_re, s0_im, nb, sp, rows_per_step):
    n, sw = u_tm.shape
    nbp, xw = s0_re.shape
    kt_n = sp["bre"].shape[0]
    r = rows_per_step
    body = functools.partial(_s5_body, nb=nb, kt_n=kt_n, lane_chunk=4 * LANE)
    return pl.pallas_call(
        body,
        grid=(n // r,),
        in_specs=[pl.BlockSpec((r, sw), lambda i: (i, 0)),
                  _const_spec((nbp, xw)), _const_spec((nbp, xw)),
                  _const_spec((1, xw)), _const_spec((1, xw)),
                  _const_spec(sp["bre"].shape), _const_spec(sp["bim"].shape),
                  _const_spec(sp["cre"].shape), _const_spec(sp["cimn"].shape),
                  _const_spec((1, sw)), _const_spec((sw, sw)), _const_spec((1, sw))],
        out_specs=[pl.BlockSpec((r, sw), lambda i: (i, 0)),
                   pl.BlockSpec((nbp, xw), lambda i: (0, 0)), pl.BlockSpec((nbp, xw), lambda i: (0, 0))],
        out_shape=[jax.ShapeDtypeStruct((n, sw), BF16),
                   jax.ShapeDtypeStruct((nbp, xw), F32), jax.ShapeDtypeStruct((nbp, xw), F32)],
        scratch_shapes=[pltpu.VMEM((r, xw), F32), pltpu.VMEM((r, xw), F32),
                        pltpu.VMEM((nbp, xw), F32), pltpu.VMEM((nbp, xw), F32)],
        compiler_params=pltpu.CompilerParams(dimension_semantics=("arbitrary",), vmem_limit_bytes=VMEM_LIMIT),
        name="s5_branch",
    )(u_tm, s0_re, s0_im, sp["a_re"], sp["a_im"], sp["bre"], sp["bim"], sp["cre"], sp["cimn"],
      sp["d"], sp["wglu"], sp["bglu"])


def _s5_setup(lam_re, lam_im, log_step, b_re, b_im, c_re, c_im, d_skip, w_glu, b_glu, kt_n=2):
    g, p = lam_re.shape
    hg = b_re.shape[2]
    ab_re, ab_im, bbt_re, bbt_im = _s5_params(lam_re, lam_im, log_step, b_re, b_im)
    gl = g // kt_n
    eye = jnp.eye(gl, dtype=F32)

    def bmat(bbt):
        return jnp.einsum("kghp,gf->kghfp", bbt.reshape(kt_n, gl, hg, p), eye).reshape(
            kt_n, gl * hg, gl * p).astype(BF16)

    def cmat(c):
        return jnp.einsum("kghp,gf->kfpgh", c.reshape(kt_n, gl, hg, p), eye).reshape(
            kt_n, gl * p, gl * hg).astype(BF16)

    return dict(a_re=ab_re.reshape(1, g * p), a_im=ab_im.reshape(1, g * p),
                bre=bmat(bbt_re), bim=bmat(bbt_im), cre=cmat(c_re), cimn=cmat(-c_im),
                d=d_skip.reshape(1, g * hg), wglu=w_glu.astype(BF16), bglu=b_glu.reshape(1, -1))


def _sortable_key(score):
    bits = lax.bitcast_convert_type(score, I32)
    return bits ^ ((bits >> 31) & 0x7FFFFFFF)


def _prompt_attn_body(qit_ref, wit_ref, qpt_ref, ki_ref, k_ref, vt_ref, lt_ref, o_ref,
                      keys_s, hi_s, lo_s, lg_s, p_s, m_s, acc_s, *, n_top, n_heads, head_dim, v_rows):
    i = pl.program_id(1)
    tk = ki_ref.shape[2]
    idim = ki_ref.shape[3]
    tq = qit_ref.shape[2]
    nkb = i + 1
    sub_iota = lax.broadcasted_iota(I32, (tk, tq), 0)
    q_pos = i * tq + lax.broadcasted_iota(I32, (tk, tq), 1)

    def score_block(j, _):
        kib = ki_ref[0, j]
        sc = None
        for h in range(IDX_HEADS):
            s = jnp.dot(kib, qit_ref[0, h * idim:(h + 1) * idim, :], preferred_element_type=F32)
            r = jnp.maximum(s, 0.0) * wit_ref[0, h:h + 1, :]
            sc = r if sc is None else sc + r
        sc = sc * (IDX_HEADS ** -0.5)
        key = jnp.where(sub_iota + j * tk <= q_pos, _sortable_key(sc), INT_MIN)
        keys_s[j] = key
        hi_s[j] = (key >> 16).astype(I16)
        lo_s[j] = ((key & 0xFFFF) + I16_MIN).astype(I16)
        return 0

    lax.fori_loop(0, nkb, score_block, 0)

    def count16(ref, pred):
        def body(j, acc):
            hit = jnp.where(pred(ref[j]), jnp.int16(1), jnp.int16(0))
            parts = [hit[r * PACK16:(r + 1) * PACK16, :] for r in range(tk // PACK16)]
            while len(parts) > 1:
                parts = [parts[a] + parts[a + 1] for a in range(0, len(parts), 2)]
            return acc + parts[0]
        acc = lax.fori_loop(0, nkb, body, jnp.zeros((PACK16, tq), I16))
        return jnp.sum(acc.astype(I32), axis=0, keepdims=True)

    def select16(ref, want):
        thr = jnp.where(count16(ref, lambda kb: kb >= jnp.int16(0)) >= want, 0, I16_MIN).astype(I32)

        def bit_step(b, thr):
            cand = thr | lax.shift_left(jnp.int32(1), 14 - b)
            c16 = cand.astype(I16)
            return jnp.where(count16(ref, lambda kb: kb >= c16) >= want, cand, thr)

        return lax.fori_loop(0, 15, bit_step, thr)

    th = select16(hi_s, n_top)
    th16 = th.astype(I16)
    want_lo = n_top - count16(hi_s, lambda kb: kb > th16)

    def mask_lo(j, _):
        lo_s[j] = jnp.where(hi_s[j] == th16, lo_s[j], jnp.int16(I16_MIN))
        return 0

    lax.fori_loop(0, nkb, mask_lo, 0)
    tl = select16(lo_s, want_lo)
    tl16 = tl.astype(I16)
    need = (want_lo - count16(lo_s, lambda kb: kb > tl16)).astype(F32)
    thr = lax.shift_left(th, 16) | (tl - I16_MIN)
    thr = jnp.maximum(thr, INT_MIN + 1)

    m_s[...] = jnp.full(m_s.shape, NEG, F32)
    acc_s[...] = jnp.zeros(acc_s.shape, F32)
    pair = 2 * head_dim

    def attend_block(j, eq_seen):
        kb = keys_s[j]
        eq = kb == thr
        eqf = jnp.where(eq, 1.0, 0.0)
        rank = jnp.dot(lt_ref[...], eqf.astype(BF16), preferred_element_type=F32) + eq_seen
        sel = (kb > thr) | (eq & (rank < need))
        madd = jnp.where(sel, 0.0, NEG)
        kblk = k_ref[0, j]
        for h in range(n_heads):
            c0 = (h // 2) * pair
            lg_s[h] = jnp.dot(kblk[:, c0:c0 + pair], qpt_ref[0, h * pair:(h + 1) * pair, :],
                              preferred_element_type=F32) + madd
        m_old = m_s[...]
        m_blk = jnp.concatenate([jnp.max(lg_s[h], axis=0, keepdims=True) for h in range(n_heads)], axis=0)
        m_new = jnp.maximum(m_old, m_blk)
        alpha = jnp.exp2(m_old - m_new)
        m_s[...] = m_new
        for h in range(n_heads):
            p_s[h] = jnp.exp2(lg_s[h] - m_new[h:h + 1, :]).astype(BF16)
        vblk = vt_ref[0, j]
        for h in range(n_heads):
            vs = slice(h * v_rows, (h + 1) * v_rows)
            acc_s[vs, :] = alpha[h:h + 1, :] * acc_s[vs, :] + jnp.dot(vblk[vs, :], p_s[h],
                                                                      preferred_element_type=F32)
        return eq_seen + jnp.sum(eqf, axis=0, keepdims=True)

    lax.fori_loop(0, nkb, attend_block, jnp.zeros((1, tq), F32))
    for h in range(n_heads):
        r0 = h * v_rows
        o_ref[0, h * head_dim:(h + 1) * head_dim, :] = (
            acc_s[r0:r0 + head_dim, :] / acc_s[r0 + head_dim:r0 + head_dim + 1, :]).astype(o_ref.dtype)


def _prompt_attention(q, kb, vb, qi, ki, wi, *, n_heads, head_dim, blk):
    b, t, aw = q.shape
    idim = ki.shape[2]
    nblk = t // blk
    n_top = min(TOP_K, t // 4)
    pair = 2 * head_dim
    qit = jnp.swapaxes(qi, 1, 2)
    wit = jnp.pad(jnp.swapaxes(wi, 1, 2), ((0, 0), (0, SUBLANE - IDX_HEADS), (0, 0)))
    q4 = q.reshape(b, t, n_heads // 2, 2, head_dim)
    sel = jnp.eye(2, dtype=q.dtype)
    qp = jnp.einsum("btgrd,rs->btgrsd", q4, sel).reshape(b, t, n_heads * pair)
    qpt = jnp.swapaxes(qp, 1, 2)
    ki_b = ki.reshape(b, nblk, blk, idim)
    k_b = kb.reshape(b, nblk, blk, aw)
    v_rows = head_dim + PACK16
    vt5 = jnp.transpose(vb.reshape(b, nblk, blk, n_heads, head_dim), (0, 1, 3, 4, 2))
    vt_b = jnp.concatenate([vt5, jnp.ones((b, nblk, n_heads, 1, blk), BF16),
                            jnp.zeros((b, nblk, n_heads, PACK16 - 1, blk), BF16)], axis=3)
    vt_b = vt_b.reshape(b, nblk, n_heads * v_rows, blk)
    lt = jnp.tril(jnp.ones((blk, blk), F32), -1).astype(BF16)

    def whole(shape):
        return pl.BlockSpec((1,) + shape, lambda bi, i: (bi, 0, 0, 0), pipeline_mode=pl.Buffered(1))

    body = functools.partial(_prompt_attn_body, n_top=n_top, n_heads=n_heads, head_dim=head_dim, v_rows=v_rows)
    out_t = pl.pallas_call(
        body,
        grid=(b, nblk),
        in_specs=[pl.BlockSpec((1, qit.shape[1], blk), lambda bi, i: (bi, 0, i)),
                  pl.BlockSpec((1, SUBLANE, blk), lambda bi, i: (bi, 0, i)),
                  pl.BlockSpec((1, qpt.shape[1], blk), lambda bi, i: (bi, 0, i)),
                  whole((nblk, blk, idim)), whole((nblk, blk, aw)), whole((nblk, n_heads * v_rows, blk)),
                  _const_spec((blk, blk))],
        out_specs=pl.BlockSpec((1, aw, blk), lambda bi, i: (bi, 0, i)),
        out_shape=jax.ShapeDtypeStruct((b, aw, t), BF16),
        scratch_shapes=[pltpu.VMEM((nblk, blk, blk), I32),
                        pltpu.VMEM((nblk, blk, blk), I16), pltpu.VMEM((nblk, blk, blk), I16),
                        pltpu.VMEM((n_heads, blk, blk), F32), pltpu.VMEM((n_heads, blk, blk), BF16),
                        pltpu.VMEM((n_heads, blk), F32),
                        pltpu.VMEM((n_heads * v_rows, blk), F32)],
        compiler_params=pltpu.CompilerParams(dimension_semantics=("arbitrary", "arbitrary"),
                                             vmem_limit_bytes=VMEM_LIMIT),
        name="prompt_attention",
    )(qit, wit, qpt, ki_b, k_b, vt_b, lt)
    return jnp.swapaxes(out_t, 1, 2)


def _sample_select_body(pt_ref, qi_ref, wi_ref, kin_ref, ut_ref, *rest, n_pages, n_top):
    page_refs = rest[:n_pages]
    madd_ref = rest[n_pages]
    keys_s = rest[n_pages + 1]
    tq = wi_ref.shape[1]
    page = page_refs[0].shape[3]
    qi = qi_ref[0]
    w = wi_ref[0]

    def tile_keys(kpage_t_bf16, allowed):
        s = jnp.dot(qi, kpage_t_bf16, preferred_element_type=F32)
        sc = None
        for h in range(IDX_HEADS):
            r = jnp.maximum(s[h * tq:(h + 1) * tq, :], 0.0) * w[:, h:h + 1]
            sc = r if sc is None else sc + r
        key = _sortable_key(sc * (IDX_HEADS ** -0.5))
        return key if allowed is None else jnp.where(allowed, key, INT_MIN)

    for p in range(n_pages):
        keys_s[p] = tile_keys(page_refs[p][0, 0].astype(BF16), None)
    t_iota = lax.broadcasted_iota(I32, (tq, page), 0)
    lane_iota = lax.broadcasted_iota(I32, (tq, page), 1)
    keys_s[n_pages] = tile_keys(kin_ref[0], lane_iota <= t_iota)
    n_tiles = n_pages + 1

    def count(pred):
        acc = jnp.zeros((tq, page), I32)
        for p in range(n_tiles):
            acc = acc + jnp.where(pred(keys_s[p]), 1, 0)
        return jnp.sum(acc, axis=1, keepdims=True)

    thr = jnp.where(count(lambda kb: kb >= 0) >= n_top, 0, INT_MIN).astype(I32)

    def bit_step(b, thr):
        cand = thr | lax.shift_left(jnp.int32(1), 30 - b)
        return jnp.where(count(lambda kb: kb >= cand) >= n_top, cand, thr)

    thr = lax.fori_loop(0, 31, bit_step, thr)
    thr = jnp.maximum(thr, INT_MIN + 1)
    need = (n_top - count(lambda kb: kb > thr)).astype(F32)

    eq_seen = jnp.zeros((tq, 1), F32)
    for p in range(n_tiles):
        kb = keys_s[p]
        eq = kb == thr
        eqf = jnp.where(eq, 1.0, 0.0)
        rank = jnp.dot(eqf.astype(BF16), ut_ref[...], preferred_element_type=F32) + eq_seen
        sel = (kb > thr) | (eq & (rank < need))
        madd_ref[0, p] = jnp.where(sel, 0.0, NEG)
        eq_seen = eq_seen + jnp.sum(eqf, axis=1, keepdims=True)


def _sample_select(page_table, qi_ht, wi, ki_new_t, cache_kidx_t, *, n_top):
    bd, n_pages = page_table.shape
    _, n_phys, idim, page = cache_kidx_t.shape
    tq = wi.shape[1]
    ut = jnp.triu(jnp.ones((page, page), F32), 1).astype(BF16)

    def page_spec(p):
        return pl.BlockSpec((1, 1, idim, page), lambda b, pt, p=p: (0, pt[b, p], 0, 0))

    grid_spec = pltpu.PrefetchScalarGridSpec(
        num_scalar_prefetch=1,
        grid=(bd,),
        in_specs=[pl.BlockSpec((1, IDX_HEADS * tq, idim), lambda b, pt: (b, 0, 0)),
                  pl.BlockSpec((1, tq, IDX_HEADS), lambda b, pt: (b, 0, 0)),
                  pl.BlockSpec((1, idim, page), lambda b, pt: (b, 0, 0)),
                  pl.BlockSpec((page, page), lambda b, pt: (0, 0))]
                 + [page_spec(p) for p in range(n_pages)],
        out_specs=pl.BlockSpec((1, n_pages + 1, tq, page), lambda b, pt: (b, 0, 0, 0)),
        scratch_shapes=[pltpu.VMEM((n_pages + 1, tq, page), I32)],
    )
    return pl.pallas_call(
        functools.partial(_sample_select_body, n_pages=n_pages, n_top=n_top),
        grid_spec=grid_spec,
        out_shape=jax.ShapeDtypeStruct((bd, n_pages + 1, tq, page), F32),
        compiler_params=pltpu.CompilerParams(dimension_semantics=("arbitrary",), vmem_limit_bytes=VMEM_LIMIT),
        name="sample_select",
    )(page_table, qi_ht, wi, ki_new_t, ut, *([cache_kidx_t] * n_pages))


def _sample_attend_body(pt_ref, qbd_ref, madd_ref, maddn_ref, kn_ref, vn_ref, *rest, pps, n_heads, head_dim):
    k_refs = rest[:pps]
    v_refs = rest[pps:2 * pps]
    o_ref = rest[2 * pps]
    m_s, l_s, acc_s = rest[2 * pps + 1:]
    c = pl.program_id(1)
    n_chunks = pl.num_programs(1)
    tq = madd_ref.shape[2]
    qbd = qbd_ref[0]
    nt = (((1,), (1,)), ((), ()))

    @pl.when(c == 0)
    def _():
        m_s[...] = jnp.full(m_s.shape, NEG, F32)
        l_s[...] = jnp.zeros(l_s.shape, F32)
        acc_s[...] = jnp.zeros(acc_s.shape, F32)

    def update(kt_tiles, vt_tiles, madd_tiles):
        lgs = []
        for kt, mt in zip(kt_tiles, madd_tiles):
            lg = jnp.dot(qbd, kt, preferred_element_type=F32)
            lgs.append(lg + jnp.concatenate([mt] * n_heads, axis=0))
        mx = lgs[0]
        for lg in lgs[1:]:
            mx = jnp.maximum(mx, lg)
        m_old = m_s[...]
        m_new = jnp.maximum(m_old, jnp.max(mx, axis=1, keepdims=True))
        alpha = jnp.exp2(m_old - m_new)
        psum = None
        pv = None
        for lg, vt in zip(lgs, vt_tiles):
            p = jnp.exp2(lg - m_new)
            psum = p if psum is None else psum + p
            d = lax.dot_general(p.astype(BF16), vt, nt, preferred_element_type=F32)
            pv = d if pv is None else pv + d
        l_s[...] = alpha * l_s[...] + jnp.sum(psum, axis=1, keepdims=True)
        acc_s[...] = alpha * acc_s[...] + pv
        m_s[...] = m_new

    update([k_refs[p][0, 0].astype(BF16) for p in range(pps)],
           [v_refs[p][0, 0].astype(BF16) for p in range(pps)],
           [madd_ref[0, p] for p in range(pps)])

    @pl.when(c == n_chunks - 1)
    def _():
        update([kn_ref[0]], [vn_ref[0]], [maddn_ref[0, 0]])
        accn = acc_s[...] / l_s[...]
        lane_head = lax.broadcasted_iota(I32, (tq, n_heads * head_dim), 1) // head_dim
        out = jnp.zeros((tq, n_heads * head_dim), F32)
        for h in range(n_heads):
            out = jnp.where(lane_head == h, accn[h * tq:(h + 1) * tq, :], out)
        o_ref[0] = out.astype(o_ref.dtype)


def _sample_attend(page_table, qbd, madd, k_new_t, v_new_t, cache_kt, cache_vt, *, n_heads, head_dim, pps):
    bd, n_pages = page_table.shape
    _, n_phys, aw, page = cache_kt.shape
    tq = madd.shape[2]
    rows = qbd.shape[1]

    def page_spec(p):
        return pl.BlockSpec((1, 1, aw, page), lambda b, c, pt, p=p: (0, pt[b, c * pps + p], 0, 0))

    grid_spec = pltpu.PrefetchScalarGridSpec(
        num_scalar_prefetch=1,
        grid=(bd, n_pages // pps),
        in_specs=[pl.BlockSpec((1, rows, aw), lambda b, c, pt: (b, 0, 0)),
                  pl.BlockSpec((1, pps, tq, page), lambda b, c, pt: (b, c, 0, 0)),
                  pl.BlockSpec((1, 1, tq, page), lambda b, c, pt: (b, n_pages, 0, 0)),
                  pl.BlockSpec((1, aw, page), lambda b, c, pt: (b, 0, 0)),
                  pl.BlockSpec((1, aw, page), lambda b, c, pt: (b, 0, 0))]
                 + [page_spec(p) for p in range(pps)] * 2,
        out_specs=pl.BlockSpec((1, tq, aw), lambda b, c, pt: (b, 0, 0)),
        scratch_shapes=[pltpu.VMEM((rows, 1), F32), pltpu.VMEM((rows, 1), F32), pltpu.VMEM((rows, aw), F32)],
    )
    return pl.pallas_call(
        functools.partial(_sample_attend_body, pps=pps, n_heads=n_heads, head_dim=head_dim),
        grid_spec=grid_spec,
        out_shape=jax.ShapeDtypeStruct((bd, tq, aw), BF16),
        compiler_params=pltpu.CompilerParams(dimension_semantics=("arbitrary", "arbitrary"),
                                             vmem_limit_bytes=VMEM_LIMIT),
        name="sample_attend",
    )(page_table, qbd, madd, madd, k_new_t, v_new_t, *([cache_kt] * pps), *([cache_vt] * pps))


def _tail_body(x_ref, ys_ref, ya_ref, gmix_ref, wg_ref, wbs_ref, wba_ref, wo_ref, gmlp_ref, wup_ref, wdn_ref,
               gfin_ref, o_ref):
    x = x_ref[...]
    d = x.shape[1]
    xn = _rms(x, gmix_ref[...]).astype(BF16)
    gates = jnp.dot(xn, wg_ref[...], preferred_element_type=F32)
    a = jnp.dot(ys_ref[...], wbs_ref[...], preferred_element_type=F32)
    b = jnp.dot(ya_ref[...], wba_ref[...], preferred_element_type=F32)
    merged = jax.nn.sigmoid(gates[:, :d]) * a + jax.nn.sigmoid(gates[:, d:]) * b
    h = x + jnp.dot(merged.astype(BF16), wo_ref[...], preferred_element_type=F32)
    hn = _rms(h, gmlp_ref[...]).astype(BF16)
    up = jnp.dot(hn, wup_ref[...], preferred_element_type=F32)
    act = jnp.square(jnp.maximum(up, 0.0)).astype(BF16)
    h = h + jnp.dot(act, wdn_ref[...], preferred_element_type=F32)
    o_ref[...] = _rms(h, gfin_ref[...])


def _tail(x2d, ys, ya, tw, *, tm):
    n, d = x2d.shape
    sw = ys.shape[1]
    aw = ya.shape[1]
    row = lambda width: pl.BlockSpec((tm, width), lambda i: (i, 0))
    consts = [tw["gmix"], tw["wg"], tw["wbs"], tw["wba"], tw["wo"], tw["gmlp"], tw["wup"], tw["wdn"], tw["gfin"]]
    return pl.pallas_call(
        _tail_body,
        grid=(n // tm,),
        in_specs=[row(d), row(sw), row(aw)] + [_const_spec(c.shape) for c in consts],
        out_specs=row(d),
        out_shape=jax.ShapeDtypeStruct((n, d), F32),
        compiler_params=pltpu.CompilerParams(dimension_semantics=("arbitrary",), vmem_limit_bytes=VMEM_LIMIT),
        name="tail",
    )(x2d, ys, ya, *consts)


def kernel(x_prompt, x_sample, cache_k, cache_v, cache_kidx, state_ssm_re, state_ssm_im, page_table, norm_mix, w_in,
           lam_re, lam_im, log_step, b_re, b_im, c_re, c_im, d_skip, w_glu, b_glu, w_branch_ssm, w_branch_attn,
           w_out, norm_mlp, w_up, w_down, norm_final):
    depth, d, _ = w_in.shape
    assert depth == 1, "single-layer step"
    l = 0
    bp, tp, _ = x_prompt.shape
    bd, ts, _ = x_sample.shape
    _, n_phys, page, n_heads, head_dim = cache_k.shape
    idim = cache_kidx.shape[3]
    g, p = lam_re.shape[1:]
    hg = b_re.shape[3]
    sw, aw, iw, xw = g * hg, n_heads * head_dim, IDX_HEADS * idim, g * p
    n_pages = page_table.shape[1]
    past_len = n_pages * page

    o_g = sw + 3 * aw + iw + idim + IDX_HEADS
    w_main = jnp.pad(w_in[l][:, :o_g], ((0, 0), (0, LANE - idim - IDX_HEADS))).astype(BF16)
    gmix = norm_mix[l].reshape(1, d)
    tw = dict(gmix=gmix, wg=w_in[l][:, o_g:].astype(BF16), wbs=w_branch_ssm[l].astype(BF16),
              wba=w_branch_attn[l].astype(BF16), wo=w_out[l].astype(BF16), gmlp=norm_mlp[l].reshape(1, d),
              wup=w_up[l].astype(BF16), wdn=w_down[l].astype(BF16), gfin=norm_final.reshape(1, d))
    sp = _s5_setup(lam_re[l], lam_im[l], log_step[l], b_re[l], b_im[l], c_re[l], c_im[l], d_skip[l],
                   w_glu[l], b_glu[l])
    proj = functools.partial(_inproj, g=gmix, w_main=w_main, sw=sw, aw=aw, iw=iw,
                             q_scale=head_dim ** -0.5 * LOG2E, qi_scale=idim ** -0.5)

    def time_major(a, b, t):
        return jnp.swapaxes(a.reshape(b, t, -1), 0, 1).reshape(t * b, -1)

    def batch_major(a, b, t):
        return jnp.swapaxes(a.reshape(t, b, -1), 0, 1).reshape(b * t, -1)

    xp = x_prompt.reshape(bp * tp, d)
    u, q, k32, kb, v32, vb, qi, kiw = proj(xp, tm=512)
    ki32 = kiw[:, :idim]
    wi = kiw[:, idim:idim + IDX_HEADS]
    nbp = max(bp, SUBLANE)
    zero_state = jnp.zeros((nbp, xw), F32)
    y_tm, sre_p, sim_p = _s5_branch(time_major(u, bp, tp), zero_state, zero_state, bp, sp, rows_per_step=512)
    y_ssm = batch_major(y_tm, bp, tp)
    y_att = _prompt_attention(q.reshape(bp, tp, aw), kb.reshape(bp, tp, aw), vb.reshape(bp, tp, aw),
                              qi.reshape(bp, tp, iw), ki32.astype(BF16).reshape(bp, tp, idim),
                              wi.reshape(bp, tp, IDX_HEADS), n_heads=n_heads, head_dim=head_dim, blk=256)
    y_prompt = _tail(xp, y_ssm, y_att.reshape(bp * tp, aw), tw, tm=256).reshape(bp, tp, d)
    k_prompt = k32.reshape(1, bp, tp, n_heads, head_dim)
    v_prompt = v32.reshape(1, bp, tp, n_heads, head_dim)
    kidx_prompt = ki32.reshape(1, bp, tp, idim)
    ssm_re_prompt = sre_p[:bp].reshape(1, bp, g, p)
    ssm_im_prompt = sim_p[:bp].reshape(1, bp, g, p)

    xs = x_sample.reshape(bd * ts, d)
    u, q, k32, kb, v32, vb, qi, kiw = proj(xs, tm=512)
    ki32 = kiw[:, :idim]
    wi = kiw[:, idim:idim + IDX_HEADS]
    y_tm, sre_s, sim_s = _s5_branch(time_major(u, bd, ts), state_ssm_re[l].reshape(bd, xw),
                                    state_ssm_im[l].reshape(bd, xw), bd, sp, rows_per_step=bd * ts)
    y_ssm = batch_major(y_tm, bd, ts)
    n_top = min(TOP_K, (past_len + ts) // 4)
    qi_ht = jnp.swapaxes(qi.reshape(bd, ts, IDX_HEADS, idim), 1, 2).reshape(bd, IDX_HEADS * ts, idim)
    new_tile_t = lambda a: jnp.swapaxes(jnp.pad(a.reshape(bd, ts, -1), ((0, 0), (0, page - ts), (0, 0))), 1, 2)
    cache_kidx_t = jnp.swapaxes(cache_kidx[l:l + 1], 2, 3)
    cache_kt = jnp.transpose(cache_k[l:l + 1], (0, 1, 3, 4, 2)).reshape(1, n_phys, aw, page)
    cache_vt = jnp.transpose(cache_v[l:l + 1], (0, 1, 3, 4, 2)).reshape(1, n_phys, aw, page)
    madd = _sample_select(page_table, qi_ht, wi.reshape(bd, ts, IDX_HEADS), new_tile_t(ki32.astype(BF16)),
                          cache_kidx_t, n_top=n_top)
    qbd = jnp.einsum("bthd,hg->bhtgd", q.reshape(bd, ts, n_heads, head_dim),
                     jnp.eye(n_heads, dtype=BF16)).reshape(bd, n_heads * ts, aw)
    y_att = _sample_attend(page_table, qbd, madd, new_tile_t(kb), new_tile_t(vb), cache_kt, cache_vt,
                           n_heads=n_heads, head_dim=head_dim, pps=8)
    y_sample = _tail(xs, y_ssm, y_att.reshape(bd * ts, aw), tw, tm=256).reshape(bd, ts, d)
    k_sample = k32.reshape(1, bd, ts, n_heads, head_dim)
    v_sample = v32.reshape(1, bd, ts, n_heads, head_dim)
    kidx_sample = ki32.reshape(1, bd, ts, idim)
    ssm_re_sample = sre_s.reshape(1, bd, g, p)
    ssm_im_sample = sim_s.reshape(1, bd, g, p)

    return (y_prompt, y_sample, k_prompt, v_prompt, kidx_prompt, ssm_re_prompt, ssm_im_prompt,
            k_sample, v_sample, kidx_sample, ssm_re_sample, ssm_im_sample)
```

```python
import functools

import jax
import jax.numpy as jnp
from jax import lax
from jax.experimental import pallas as pl
from jax.experimental.pallas import tpu as pltpu

F32 = jnp.float32
BF16 = jnp.bfloat16
I32 = jnp.int32
I16 = jnp.int16

EPS = 1e-6
TOP_K = 256
IDX_HEADS = 4
INT_MIN = -(2 ** 31)
I16_MIN = -(2 ** 15)
KEY_NEG_INF = 0x807FFFFF - 2 ** 32
FLT_MAX = 3.4028234663852886e38
MAX_SETTLE_STEPS = 64
LOG2E = 1.4426950408889634
NEG = -1e30
LANE = 128
SUBLANE = 8
PACK16 = 16
VMEM_LIMIT = 56 * 1024 * 1024


def _const_spec(shape):
    nd = len(shape)
    return pl.BlockSpec(shape, lambda *_: (0,) * nd, pipeline_mode=pl.Buffered(1))


def _rms(x, g):
    return x * lax.rsqrt(jnp.mean(x * x, axis=-1, keepdims=True) + EPS) * g


def _inproj_body(x_ref, g_ref, w_ref, u_ref, q_ref, k_ref, kb_ref, v_ref, vb_ref, qi_ref, kiw_ref,
                 *, sw, aw, iw, q_scale, qi_scale):
    xn = _rms(x_ref[...], g_ref[...]).astype(BF16)
    p = jnp.dot(xn, w_ref[...], preferred_element_type=F32)
    o = 0
    u_ref[...] = p[:, o:o + sw]
    o += sw
    q_ref[...] = (p[:, o:o + aw] * q_scale).astype(BF16)
    o += aw
    k = p[:, o:o + aw]
    k_ref[...] = k
    kb_ref[...] = k.astype(BF16)
    o += aw
    v = p[:, o:o + aw]
    v_ref[...] = v
    vb_ref[...] = v.astype(BF16)
    o += aw
    qi_ref[...] = (p[:, o:o + iw] * qi_scale).astype(BF16)
    o += iw
    kiw_ref[...] = p[:, o:o + LANE]


def _inproj(x2d, g, w_main, *, sw, aw, iw, q_scale, qi_scale, tm):
    n, d = x2d.shape
    wtot = w_main.shape[1]
    row = lambda width: pl.BlockSpec((tm, width), lambda i: (i, 0))
    return pl.pallas_call(
        functools.partial(_inproj_body, sw=sw, aw=aw, iw=iw, q_scale=q_scale, qi_scale=qi_scale),
        grid=(n // tm,),
        in_specs=[row(d), _const_spec((1, d)), _const_spec((d, wtot))],
        out_specs=[row(sw), row(aw), row(aw), row(aw), row(aw), row(aw), row(iw), row(LANE)],
        out_shape=[
            jax.ShapeDtypeStruct((n, sw), F32),
            jax.ShapeDtypeStruct((n, aw), BF16),
            jax.ShapeDtypeStruct((n, aw), F32),
            jax.ShapeDtypeStruct((n, aw), BF16),
            jax.ShapeDtypeStruct((n, aw), F32),
            jax.ShapeDtypeStruct((n, aw), BF16),
            jax.ShapeDtypeStruct((n, iw), BF16),
            jax.ShapeDtypeStruct((n, LANE), F32),
        ],
        compiler_params=pltpu.CompilerParams(dimension_semantics=("arbitrary",), vmem_limit_bytes=VMEM_LIMIT),
        name="inproj",
    )(x2d, g, w_main)


def _s5_param_body(lre_ref, lim_ref, ls_ref, bre_ref, bim_ref, are_ref, aim_ref, bbre_ref, bbim_ref):
    lam_re = lre_ref[...]
    lam_im = lim_ref[...]
    dt = jnp.exp(ls_ref[...])
    mag = jnp.exp(lam_re * dt)
    ang = lam_im * dt
    ab_re = mag * jnp.cos(ang)
    ab_im = mag * jnp.sin(ang)
    nr = ab_re - 1.0
    ni = ab_im
    den = lam_re * lam_re + lam_im * lam_im
    z_re = (nr * lam_re + ni * lam_im) / den
    z_im = (ni * lam_re - nr * lam_im) / den
    are_ref[...] = ab_re
    aim_ref[...] = ab_im
    b_re = bre_ref[...]
    b_im = bim_ref[...]
    bbre_ref[...] = z_re[:, None, :] * b_re - z_im[:, None, :] * b_im
    bbim_ref[...] = z_re[:, None, :] * b_im + z_im[:, None, :] * b_re


def _s5_params(lam_re, lam_im, log_step, b_re, b_im):
    g, p = lam_re.shape
    hg = b_re.shape[2]
    bt_re = jnp.swapaxes(b_re, 1, 2)
    bt_im = jnp.swapaxes(b_im, 1, 2)
    return pl.pallas_call(
        _s5_param_body,
        out_shape=[jax.ShapeDtypeStruct((g, p), F32), jax.ShapeDtypeStruct((g, p), F32),
                   jax.ShapeDtypeStruct((g, hg, p), F32), jax.ShapeDtypeStruct((g, hg, p), F32)],
        name="s5_params",
    )(lam_re, lam_im, log_step.reshape(g, 1), bt_re, bt_im)


def _s5_body(u_ref, s0re_ref, s0im_ref, are_ref, aim_ref, bre_ref, bim_ref, cre_ref, cimn_ref, d_ref,
             wglu_ref, bglu_ref, y_ref, sre_ref, sim_ref, xre, xim, car_re, car_im, *, nb, kt_n, lane_chunk):
    step = pl.program_id(0)
    rows, sw = u_ref.shape
    xw = xre.shape[1]
    kw = sw // kt_n
    xk = xw // kt_n

    @pl.when(step == 0)
    def _():
        car_re[...] = s0re_ref[...]
        car_im[...] = s0im_ref[...]

    u = u_ref[...]
    ub = u.astype(BF16)
    for kt in range(kt_n):
        xre[:, kt * xk:(kt + 1) * xk] = jnp.dot(ub[:, kt * kw:(kt + 1) * kw], bre_ref[kt],
                                                preferred_element_type=F32)
        xim[:, kt * xk:(kt + 1) * xk] = jnp.dot(ub[:, kt * kw:(kt + 1) * kw], bim_ref[kt],
                                                preferred_element_type=F32)

    if nb < SUBLANE:
        assert 2 * nb == SUBLANE
        lo = lax.broadcasted_iota(I32, (SUBLANE, lane_chunk), 0) < nb
        for c in range(xw // lane_chunk):
            cs = slice(c * lane_chunk, (c + 1) * lane_chunk)
            a_r = jnp.broadcast_to(are_ref[:, cs], (SUBLANE, lane_chunk))
            a_i = jnp.broadcast_to(aim_ref[:, cs], (SUBLANE, lane_chunk))

            def tile_step(j, carry, cs=cs, a_r=a_r, a_i=a_i):
                cr, ci = carry
                r0 = pl.multiple_of(j * SUBLANE, SUBLANE)
                tr = xre[pl.ds(r0, SUBLANE), cs]
                ti = xim[pl.ds(r0, SUBLANE), cs]
                x1r = a_r * cr - a_i * ci + tr
                x1i = a_r * ci + a_i * cr + ti
                sr = pltpu.roll(x1r, nb, 0)
                si = pltpu.roll(x1i, nb, 0)
                x2r = a_r * sr - a_i * si + tr
                x2i = a_r * si + a_i * sr + ti
                xre[pl.ds(r0, SUBLANE), cs] = jnp.where(lo, x1r, x2r)
                xim[pl.ds(r0, SUBLANE), cs] = jnp.where(lo, x1i, x2i)
                return pltpu.roll(x2r, nb, 0), pltpu.roll(x2i, nb, 0)

            cr, ci = lax.fori_loop(0, rows // SUBLANE, tile_step, (car_re[:, cs], car_im[:, cs]))
            car_re[:, cs] = cr
            car_im[:, cs] = ci
    else:
        a_r = are_ref[...]
        a_i = aim_ref[...]
        for t in range(rows // nb):
            rs = slice(t * nb, (t + 1) * nb)
            if t == 0:
                pr, pi = car_re[...], car_im[...]
            else:
                ps = slice((t - 1) * nb, t * nb)
                pr, pi = xre[ps, :], xim[ps, :]
            xre[rs, :] = a_r * pr - a_i * pi + xre[rs, :]
            xim[rs, :] = a_r * pi + a_i * pr + xim[rs, :]
        ls = slice(rows - nb, rows)
        car_re[...] = xre[ls, :]
        car_im[...] = xim[ls, :]

    sre_ref[...] = car_re[...]
    sim_ref[...] = car_im[...]

    ys = []
    for kt in range(kt_n):
        xs = slice(kt * xk, (kt + 1) * xk)
        ys.append(jnp.dot(xre[:, xs].astype(BF16), cre_ref[kt], preferred_element_type=F32)
                  + jnp.dot(xim[:, xs].astype(BF16), cimn_ref[kt], preferred_element_type=F32))
    y = jnp.concatenate(ys, axis=1) + d_ref[...] * u
    y = jax.nn.gelu(y)
    gate = jnp.dot(y.astype(BF16), wglu_ref[...], preferred_element_type=F32) + bglu_ref[...]
    y_ref[...] = (y * jax.nn.sigmoid(gate)).astype(BF16)


def _s5_branch(u_tm, s0_re, s0_im, nb, sp, rows_per_step):
    n, sw = u_tm.shape
    nbp, xw = s0_re.shape
    kt_n = sp["bre"].shape[0]
    r = rows_per_step
    body = functools.partial(_s5_body, nb=nb, kt_n=kt_n, lane_chunk=4 * LANE)
    return pl.pallas_call(
        body,
        grid=(n // r,),
        in_specs=[pl.BlockSpec((r, sw), lambda i: (i, 0)),
                  _const_spec((nbp, xw)), _const_spec((nbp, xw)),
                  _const_spec((1, xw)), _const_spec((1, xw)),
                  _const_spec(sp["bre"].shape), _const_spec(sp["bim"].shape),
                  _const_spec(sp["cre"].shape), _const_spec(sp["cimn"].shape),
                  _const_spec((1, sw)), _const_spec((sw, sw)), _const_spec((1, sw))],
        out_specs=[pl.BlockSpec((r, sw), lambda i: (i, 0)),
                   pl.BlockSpec((nbp, xw), lambda i: (0, 0)), pl.BlockSpec((nbp, xw), lambda i: (0, 0))],
        out_shape=[jax.ShapeDtypeStruct((n, sw), BF16),
                   jax.ShapeDtypeStruct((nbp, xw), F32), jax.ShapeDtypeStruct((nbp, xw), F32)],
        scratch_shapes=[pltpu.VMEM((r, xw), F32), pltpu.VMEM((r, xw), F32),
                        pltpu.VMEM((nbp, xw), F32), pltpu.VMEM((nbp, xw), F32)],
        compiler_params=pltpu.CompilerParams(dimension_semantics=("arbitrary",), vmem_limit_bytes=VMEM_LIMIT),
        name="s5_branch",
    )(u_tm, s0_re, s0_im, sp["a_re"], sp["a_im"], sp["bre"], sp["bim"], sp["cre"], sp["cimn"],
      sp["d"], sp["wglu"], sp["bglu"])


def _s5_setup(lam_re, lam_im, log_step, b_re, b_im, c_re, c_im, d_skip, w_glu, b_glu, kt_n=2):
    g, p = lam_re.shape
    hg = b_re.shape[2]
    ab_re, ab_im, bbt_re, bbt_im = _s5_params(lam_re, lam_im, log_step, b_re, b_im)
    gl = g // kt_n
    eye = jnp.eye(gl, dtype=F32)

    def bmat(bbt):
        return jnp.einsum("kghp,gf->kghfp", bbt.reshape(kt_n, gl, hg, p), eye).reshape(
            kt_n, gl * hg, gl * p).astype(BF16)

    def cmat(c):
        return jnp.einsum("kghp,gf->kfpgh", c.reshape(kt_n, gl, hg, p), eye).reshape(
            kt_n, gl * p, gl * hg).astype(BF16)

    return dict(a_re=ab_re.reshape(1, g * p), a_im=ab_im.reshape(1, g * p),
                bre=bmat(bbt_re), bim=bmat(bbt_im), cre=cmat(c_re), cimn=cmat(-c_im),
                d=d_skip.reshape(1, g * hg), wglu=w_glu.astype(BF16), bglu=b_glu.reshape(1, -1))


def _sortable_key(score):
    bits = lax.bitcast_convert_type(score, I32)
    return bits ^ ((bits >> 31) & 0x7FFFFFFF)


def _key_to_score(key):
    key = jnp.maximum(key, KEY_NEG_INF)
    return lax.bitcast_convert_type(key ^ ((key >> 31) & 0x7FFFFFFF), F32)


def _settle_threshold(v, n_top, count_gt_ge, neighbours):
    def flag(c_gt, c_ge):
        return jnp.max(jnp.where((c_gt >= n_top) | (c_ge < n_top), 1, 0))

    def body(st):
        it, v, c_gt, c_ge, _ = st
        up, dn = neighbours(v)
        v = jnp.where(c_gt >= n_top, up, jnp.where(c_ge < n_top, dn, v))
        c_gt, c_ge = count_gt_ge(v)
        return it + 1, v, c_gt, c_ge, flag(c_gt, c_ge)

    c_gt, c_ge = count_gt_ge(v)
    st = lax.while_loop(lambda st: (st[0] < MAX_SETTLE_STEPS) & (st[4] > 0), body,
                        (jnp.int32(0), v, c_gt, c_ge, flag(c_gt, c_ge)))
    return st[1], st[2]


def _prompt_attn_body(qit_ref, wit_ref, qpt_ref, ki_ref, k_ref, vt_ref, lt_ref, o_ref,
                      sc_s, hi_s, lo_s, lg_s, p_s, m_s, alpha_s, acc_s, *, n_top, n_heads, head_dim, v_rows):
    i = pl.program_id(1)
    tk = ki_ref.shape[2]
    idim = ki_ref.shape[3]
    tq = qit_ref.shape[2]
    nkb = i + 1
    sub_iota = lax.broadcasted_iota(I32, (tk, tq), 0)
    q_pos = i * tq + lax.broadcasted_iota(I32, (tk, tq), 1)

    def score_block(j, _):
        kib = ki_ref[0, j]
        sc = None
        for h in range(IDX_HEADS):
            s = jnp.dot(kib, qit_ref[0, h * idim:(h + 1) * idim, :], preferred_element_type=F32)
            r = jnp.maximum(s, 0.0) * wit_ref[0, h:h + 1, :]
            sc = r if sc is None else sc + r
        sc = sc * (IDX_HEADS ** -0.5)
        allowed = sub_iota + j * tk <= q_pos
        sc_s[j] = jnp.where(allowed, sc, -jnp.inf)
        key = jnp.where(allowed, _sortable_key(sc), INT_MIN)
        hi_s[j] = (key >> 16).astype(I16)
        lo_s[j] = ((key & 0xFFFF) + I16_MIN).astype(I16)
        return 0

    lax.fori_loop(0, nkb, score_block, 0)

    def count16(ref, pred):
        def body(j, acc):
            hit = jnp.where(pred(ref[j]), jnp.int16(1), jnp.int16(0))
            parts = [hit[r * PACK16:(r + 1) * PACK16, :] for r in range(tk // PACK16)]
            while len(parts) > 1:
                parts = [parts[a] + parts[a + 1] for a in range(0, len(parts), 2)]
            return acc + parts[0]
        acc = lax.fori_loop(0, nkb, body, jnp.zeros((PACK16, tq), I16))
        return jnp.sum(acc.astype(I32), axis=0, keepdims=True)

    def select16(ref, want):
        thr = jnp.where(count16(ref, lambda kb: kb >= jnp.int16(0)) >= want, 0, I16_MIN).astype(I32)

        def bit_step(b, thr):
            cand = thr | lax.shift_left(jnp.int32(1), 14 - b)
            c16 = cand.astype(I16)
            return jnp.where(count16(ref, lambda kb: kb >= c16) >= want, cand, thr)

        return lax.fori_loop(0, 15, bit_step, thr)

    th = select16(hi_s, n_top)
    th16 = th.astype(I16)
    want_lo = n_top - count16(hi_s, lambda kb: kb > th16)

    def mask_lo(j, _):
        lo_s[j] = jnp.where(hi_s[j] == th16, lo_s[j], jnp.int16(I16_MIN))
        return 0

    lax.fori_loop(0, nkb, mask_lo, 0)
    tl = select16(lo_s, want_lo)
    thr_score = _key_to_score(lax.shift_left(th, 16) | (tl - I16_MIN))

    def count_gt_ge(v):
        def body(j, acc):
            s = sc_s[j]
            gt = jnp.where(s > v, 1, 0).reshape(tk // SUBLANE, SUBLANE, tq).sum(axis=0)
            ge = jnp.where(s >= v, 1, 0).reshape(tk // SUBLANE, SUBLANE, tq).sum(axis=0)
            return acc[0] + gt, acc[1] + ge
        zero = jnp.zeros((SUBLANE, tq), I32)
        gt, ge = lax.fori_loop(0, nkb, body, (zero, zero))
        return gt.sum(axis=0, keepdims=True), ge.sum(axis=0, keepdims=True)

    def neighbours(v):
        def body(j, acc):
            s = sc_s[j]
            up = jnp.where(s > v, s, jnp.inf).reshape(tk // SUBLANE, SUBLANE, tq).min(axis=0)
            dn = jnp.where(s < v, s, -jnp.inf).reshape(tk // SUBLANE, SUBLANE, tq).max(axis=0)
            return jnp.minimum(acc[0], up), jnp.maximum(acc[1], dn)
        up, dn = lax.fori_loop(0, nkb, body, (jnp.full((SUBLANE, tq), jnp.inf, F32),
                                               jnp.full((SUBLANE, tq), -jnp.inf, F32)))
        return up.min(axis=0, keepdims=True), dn.max(axis=0, keepdims=True)

    thr_score, n_gt = _settle_threshold(thr_score, n_top, count_gt_ge, neighbours)
    need = (n_top - n_gt).astype(F32)
    thr_score = jnp.maximum(thr_score, -FLT_MAX)

    m_s[...] = jnp.full(m_s.shape, NEG, F32)
    acc_s[...] = jnp.zeros(acc_s.shape, F32)
    pair = 2 * head_dim

    def attend_block(j, eq_seen):
        s = sc_s[j]
        eq = s == thr_score
        eqf = jnp.where(eq, 1.0, 0.0)
        rank = jnp.dot(lt_ref[...], eqf.astype(BF16), preferred_element_type=F32) + eq_seen
        sel = (s > thr_score) | (eq & (rank < need))
        madd = jnp.where(sel, 0.0, NEG)
        kblk = k_ref[0, j]
        vblk = vt_ref[0, j]

        def logits(h):
            c0 = (h // 2) * pair
            lg_s[h] = jnp.dot(kblk[:, c0:c0 + pair], qpt_ref[0, h * pair:(h + 1) * pair, :],
                              preferred_element_type=F32) + madd

        def softmax(h):
            m_old = m_s[h:h + 1, :]
            m_new = jnp.maximum(m_old, jnp.max(lg_s[h], axis=0, keepdims=True))
            alpha_s[h:h + 1, :] = jnp.exp2(m_old - m_new)
            m_s[h:h + 1, :] = m_new
            p_s[h] = jnp.exp2(lg_s[h] - m_new).astype(BF16)

        def apply_pv(h):
            vs = slice(h * v_rows, (h + 1) * v_rows)
            acc_s[vs, :] = alpha_s[h:h + 1, :] * acc_s[vs, :] + jnp.dot(vblk[vs, :], p_s[h],
                                                                        preferred_element_type=F32)

        for stage in (logits, softmax, apply_pv):
            for h in range(n_heads):
                stage(h)
        return eq_seen + jnp.sum(eqf, axis=0, keepdims=True)

    lax.fori_loop(0, nkb, attend_block, jnp.zeros((1, tq), F32))
    for h in range(n_heads):
        r0 = h * v_rows
        o_ref[0, h * head_dim:(h + 1) * head_dim, :] = (
            acc_s[r0:r0 + head_dim, :] / acc_s[r0 + head_dim:r0 + head_dim + 1, :]).astype(o_ref.dtype)


def _prompt_attention(q, kb, vb, qi, ki, wi, *, n_heads, head_dim, blk):
    b, t, aw = q.shape
    idim = ki.shape[2]
    nblk = t // blk
    n_top = min(TOP_K, t // 4)
    pair = 2 * head_dim
    qit = jnp.swapaxes(qi, 1, 2)
    wit = jnp.pad(jnp.swapaxes(wi, 1, 2), ((0, 0), (0, SUBLANE - IDX_HEADS), (0, 0)))
    q4 = q.reshape(b, t, n_heads // 2, 2, head_dim)
    sel = jnp.eye(2, dtype=q.dtype)
    qp = jnp.einsum("btgrd,rs->btgrsd", q4, sel).reshape(b, t, n_heads * pair)
    qpt = jnp.swapaxes(qp, 1, 2)
    ki_b = ki.reshape(b, nblk, blk, idim)
    k_b = kb.reshape(b, nblk, blk, aw)
    v_rows = head_dim + PACK16
    vt5 = jnp.transpose(vb.reshape(b, nblk, blk, n_heads, head_dim), (0, 1, 3, 4, 2))
    vt_b = jnp.concatenate([vt5, jnp.ones((b, nblk, n_heads, 1, blk), BF16),
                            jnp.zeros((b, nblk, n_heads, PACK16 - 1, blk), BF16)], axis=3)
    vt_b = vt_b.reshape(b, nblk, n_heads * v_rows, blk)
    lt = jnp.tril(jnp.ones((blk, blk), F32), -1).astype(BF16)

    def whole(shape):
        return pl.BlockSpec((1,) + shape, lambda bi, i: (bi, 0, 0, 0), pipeline_mode=pl.Buffered(1))

    body = functools.partial(_prompt_attn_body, n_top=n_top, n_heads=n_heads, head_dim=head_dim, v_rows=v_rows)
    out_t = pl.pallas_call(
        body,
        grid=(b, nblk),
        in_specs=[pl.BlockSpec((1, qit.shape[1], blk), lambda bi, i: (bi, 0, i)),
                  pl.BlockSpec((1, SUBLANE, blk), lambda bi, i: (bi, 0, i)),
                  pl.BlockSpec((1, qpt.shape[1], blk), lambda bi, i: (bi, 0, i)),
                  whole((nblk, blk, idim)), whole((nblk, blk, aw)), whole((nblk, n_heads * v_rows, blk)),
                  _const_spec((blk, blk))],
        out_specs=pl.BlockSpec((1, aw, blk), lambda bi, i: (bi, 0, i)),
        out_shape=jax.ShapeDtypeStruct((b, aw, t), BF16),
        scratch_shapes=[pltpu.VMEM((nblk, blk, blk), F32),
                        pltpu.VMEM((nblk, blk, blk), I16), pltpu.VMEM((nblk, blk, blk), I16),
                        pltpu.VMEM((n_heads, blk, blk), F32), pltpu.VMEM((n_heads, blk, blk), BF16),
                        pltpu.VMEM((n_heads, blk), F32), pltpu.VMEM((n_heads, blk), F32),
                        pltpu.VMEM((n_heads * v_rows, blk), F32)],
        compiler_params=pltpu.CompilerParams(dimension_semantics=("arbitrary", "arbitrary"),
                                             vmem_limit_bytes=VMEM_LIMIT),
        name="prompt_attention",
    )(qit, wit, qpt, ki_b, k_b, vt_b, lt)
    return jnp.swapaxes(out_t, 1, 2)


def _sample_select_body(pt_ref, qi_ref, wi_ref, kin_ref, ut_ref, *rest, n_pages, n_top):
    page_refs = rest[:n_pages]
    madd_ref = rest[n_pages]
    keys_s, sc_s = rest[n_pages + 1:]
    tq = wi_ref.shape[1]
    page = page_refs[0].shape[3]
    qi = qi_ref[0]
    w = wi_ref[0]

    def score_tile(p, kpage_t_bf16, allowed):
        s = jnp.dot(qi, kpage_t_bf16, preferred_element_type=F32)
        sc = None
        for h in range(IDX_HEADS):
            r = jnp.maximum(s[h * tq:(h + 1) * tq, :], 0.0) * w[:, h:h + 1]
            sc = r if sc is None else sc + r
        sc = sc * (IDX_HEADS ** -0.5)
        key = _sortable_key(sc)
        if allowed is not None:
            sc = jnp.where(allowed, sc, -jnp.inf)
            key = jnp.where(allowed, key, INT_MIN)
        sc_s[p] = sc
        keys_s[p] = key

    for p in range(n_pages):
        score_tile(p, page_refs[p][0, 0].astype(BF16), None)
    t_iota = lax.broadcasted_iota(I32, (tq, page), 0)
    lane_iota = lax.broadcasted_iota(I32, (tq, page), 1)
    score_tile(n_pages, kin_ref[0], lane_iota <= t_iota)
    n_tiles = n_pages + 1

    def count(preds):
        accs = [jnp.zeros((tq, page), I32) for _ in preds]
        for p in range(n_tiles):
            kb = keys_s[p]
            accs = [a + jnp.where(pred(kb), 1, 0) for a, pred in zip(accs, preds)]
        return [jnp.sum(a, axis=1, keepdims=True) for a in accs]

    def two_bit_step(s, thr):
        b_hi = lax.shift_left(jnp.int32(1), 31 - 2 * s)
        b_lo = lax.shift_left(jnp.int32(1), 30 - 2 * s)
        c1, c2, c3 = thr + b_lo, thr + b_hi, thr + b_hi + b_lo
        n1, n2, n3 = count([lambda kb: kb >= c1, lambda kb: kb >= c2, lambda kb: kb >= c3])
        return jnp.where(n3 >= n_top, c3, jnp.where(n2 >= n_top, c2, jnp.where(n1 >= n_top, c1, thr)))

    thr = lax.fori_loop(0, 16, two_bit_step, jnp.full((tq, 1), INT_MIN, I32))

    def count_gt_ge(v):
        gt = jnp.zeros((tq, page), I32)
        ge = jnp.zeros((tq, page), I32)
        for p in range(n_tiles):
            s = sc_s[p]
            gt = gt + jnp.where(s > v, 1, 0)
            ge = ge + jnp.where(s >= v, 1, 0)
        return jnp.sum(gt, axis=1, keepdims=True), jnp.sum(ge, axis=1, keepdims=True)

    def neighbours(v):
        up = jnp.full((tq, page), jnp.inf, F32)
        dn = jnp.full((tq, page), -jnp.inf, F32)
        for p in range(n_tiles):
            s = sc_s[p]
            up = jnp.minimum(up, jnp.where(s > v, s, jnp.inf))
            dn = jnp.maximum(dn, jnp.where(s < v, s, -jnp.inf))
        return jnp.min(up, axis=1, keepdims=True), jnp.max(dn, axis=1, keepdims=True)

    thr_score, n_gt = _settle_threshold(_key_to_score(thr), n_top, count_gt_ge, neighbours)
    need = (n_top - n_gt).astype(F32)
    thr_score = jnp.maximum(thr_score, -FLT_MAX)

    eq_seen = jnp.zeros((tq, 1), F32)
    for p in range(n_tiles):
        s = sc_s[p]
        eq = s == thr_score
        eqf = jnp.where(eq, 1.0, 0.0)
        rank = jnp.dot(eqf.astype(BF16), ut_ref[...], preferred_element_type=F32) + eq_seen
        sel = (s > thr_score) | (eq & (rank < need))
        madd_ref[0, p] = jnp.where(sel, 0.0, NEG)
        eq_seen = eq_seen + jnp.sum(eqf, axis=1, keepdims=True)


def _sample_select(page_table, qi_ht, wi, ki_new_t, cache_kidx_t, *, n_top):
    bd, n_pages = page_table.shape
    _, n_phys, idim, page = cache_kidx_t.shape
    tq = wi.shape[1]
    ut = jnp.triu(jnp.ones((page, page), F32), 1).astype(BF16)

    def page_spec(p):
        return pl.BlockSpec((1, 1, idim, page), lambda b, pt, p=p: (0, pt[b, p], 0, 0))

    grid_spec = pltpu.PrefetchScalarGridSpec(
        num_scalar_prefetch=1,
        grid=(bd,),
        in_specs=[pl.BlockSpec((1, IDX_HEADS * tq, idim), lambda b, pt: (b, 0, 0)),
                  pl.BlockSpec((1, tq, IDX_HEADS), lambda b, pt: (b, 0, 0)),
                  pl.BlockSpec((1, idim, page), lambda b, pt: (b, 0, 0)),
                  pl.BlockSpec((page, page), lambda b, pt: (0, 0))]
                 + [page_spec(p) for p in range(n_pages)],
        out_specs=pl.BlockSpec((1, n_pages + 1, tq, page), lambda b, pt: (b, 0, 0, 0)),
        scratch_shapes=[pltpu.VMEM((n_pages + 1, tq, page), I32), pltpu.VMEM((n_pages + 1, tq, page), F32)],
    )
    return pl.pallas_call(
        functools.partial(_sample_select_body, n_pages=n_pages, n_top=n_top),
        grid_spec=grid_spec,
        out_shape=jax.ShapeDtypeStruct((bd, n_pages + 1, tq, page), F32),
        compiler_params=pltpu.CompilerParams(dimension_semantics=("arbitrary",), vmem_limit_bytes=VMEM_LIMIT),
        name="sample_select",
    )(page_table, qi_ht, wi, ki_new_t, ut, *([cache_kidx_t] * n_pages))


def _sample_attend_body(pt_ref, qbd_ref, madd_ref, maddn_ref, kn_ref, vn_ref, *rest, pps, n_heads, head_dim):
    k_refs = rest[:pps]
    v_refs = rest[pps:2 * pps]
    o_ref = rest[2 * pps]
    m_s, l_s, acc_s = rest[2 * pps + 1:]
    c = pl.program_id(1)
    n_chunks = pl.num_programs(1)
    tq = madd_ref.shape[2]
    qbd = qbd_ref[0]
    nt = (((1,), (1,)), ((), ()))

    @pl.when(c == 0)
    def _():
        m_s[...] = jnp.full(m_s.shape, NEG, F32)
        l_s[...] = jnp.zeros(l_s.shape, F32)
        acc_s[...] = jnp.zeros(acc_s.shape, F32)

    def update(kt_tiles, vt_tiles, madd_tiles):
        lgs = []
        for kt, mt in zip(kt_tiles, madd_tiles):
            lg = jnp.dot(qbd, kt, preferred_element_type=F32)
            lgs.append(lg + jnp.concatenate([mt] * n_heads, axis=0))
        mx = lgs[0]
        for lg in lgs[1:]:
            mx = jnp.maximum(mx, lg)
        m_old = m_s[...]
        m_new = jnp.maximum(m_old, jnp.max(mx, axis=1, keepdims=True))
        alpha = jnp.exp2(m_old - m_new)
        psum = None
        pv = None
        for lg, vt in zip(lgs, vt_tiles):
            p = jnp.exp2(lg - m_new)
            psum = p if psum is None else psum + p
            d = lax.dot_general(p.astype(BF16), vt, nt, preferred_element_type=F32)
            pv = d if pv is None else pv + d
        l_s[...] = alpha * l_s[...] + jnp.sum(psum, axis=1, keepdims=True)
        acc_s[...] = alpha * acc_s[...] + pv
        m_s[...] = m_new

    update([k_refs[p][0, 0].astype(BF16) for p in range(pps)],
           [v_refs[p][0, 0].astype(BF16) for p in range(pps)],
           [madd_ref[0, p] for p in range(pps)])

    @pl.when(c == n_chunks - 1)
    def _():
        update([kn_ref[0]], [vn_ref[0]], [maddn_ref[0, 0]])
        accn = acc_s[...] / l_s[...]
        lane_head = lax.broadcasted_iota(I32, (tq, n_heads * head_dim), 1) // head_dim
        out = jnp.zeros((tq, n_heads * head_dim), F32)
        for h in range(n_heads):
            out = jnp.where(lane_head == h, accn[h * tq:(h + 1) * tq, :], out)
        o_ref[0] = out.astype(o_ref.dtype)


def _sample_attend(page_table, qbd, madd, k_new_t, v_new_t, cache_kt, cache_vt, *, n_heads, head_dim, pps):
    bd, n_pages = page_table.shape
    _, n_phys, aw, page = cache_kt.shape
    tq = madd.shape[2]
    rows = qbd.shape[1]

    def page_spec(p):
        return pl.BlockSpec((1, 1, aw, page), lambda b, c, pt, p=p: (0, pt[b, c * pps + p], 0, 0))

    grid_spec = pltpu.PrefetchScalarGridSpec(
        num_scalar_prefetch=1,
        grid=(bd, n_pages // pps),
        in_specs=[pl.BlockSpec((1, rows, aw), lambda b, c, pt: (b, 0, 0)),
                  pl.BlockSpec((1, pps, tq, page), lambda b, c, pt: (b, c, 0, 0)),
                  pl.BlockSpec((1, 1, tq, page), lambda b, c, pt: (b, n_pages, 0, 0)),
                  pl.BlockSpec((1, aw, page), lambda b, c, pt: (b, 0, 0)),
                  pl.BlockSpec((1, aw, page), lambda b, c, pt: (b, 0, 0))]
                 + [page_spec(p) for p in range(pps)] * 2,
        out_specs=pl.BlockSpec((1, tq, aw), lambda b, c, pt: (b, 0, 0)),
        scratch_shapes=[pltpu.VMEM((rows, 1), F32), pltpu.VMEM((rows, 1), F32), pltpu.VMEM((rows, aw), F32)],
    )
    return pl.pallas_call(
        functools.partial(_sample_attend_body, pps=pps, n_heads=n_heads, head_dim=head_dim),
        grid_spec=grid_spec,
        out_shape=jax.ShapeDtypeStruct((bd, tq, aw), BF16),
        compiler_params=pltpu.CompilerParams(dimension_semantics=("arbitrary", "arbitrary"),
                                             vmem_limit_bytes=VMEM_LIMIT),
        name="sample_attend",
    )(page_table, qbd, madd, madd, k_new_t, v_new_t, *([cache_kt] * pps), *([cache_vt] * pps))


def _tail_body(x_ref, ys_ref, ya_ref, gmix_ref, wg_ref, wbs_ref, wba_ref, wo_ref, gmlp_ref, wup_ref, wdn_ref,
               gfin_ref, o_ref):
    x = x_ref[...]
    d = x.shape[1]
    xn = _rms(x, gmix_ref[...]).astype(BF16)
    gates = jnp.dot(xn, wg_ref[...], preferred_element_type=F32)
    a = jnp.dot(ys_ref[...], wbs_ref[...], preferred_element_type=F32)
    b = jnp.dot(ya_ref[...], wba_ref[...], preferred_element_type=F32)
    merged = jax.nn.sigmoid(gates[:, :d]) * a + jax.nn.sigmoid(gates[:, d:]) * b
    h = x + jnp.dot(merged.astype(BF16), wo_ref[...], preferred_element_type=F32)
    hn = _rms(h, gmlp_ref[...]).astype(BF16)
    up = jnp.dot(hn, wup_ref[...], preferred_element_type=F32)
    act = jnp.square(jnp.maximum(up, 0.0)).astype(BF16)
    h = h + jnp.dot(act, wdn_ref[...], preferred_element_type=F32)
    o_ref[...] = _rms(h, gfin_ref[...])


def _tail(x2d, ys, ya, tw, *, tm):
    n, d = x2d.shape
    sw = ys.shape[1]
    aw = ya.shape[1]
    row = lambda width: pl.BlockSpec((tm, width), lambda i: (i, 0))
    consts = [tw["gmix"], tw["wg"], tw["wbs"], tw["wba"], tw["wo"], tw["gmlp"], tw["wup"], tw["wdn"], tw["gfin"]]
    return pl.pallas_call(
        _tail_body,
        grid=(n // tm,),
        in_specs=[row(d), row(sw), row(aw)] + [_const_spec(c.shape) for c in consts],
        out_specs=row(d),
        out_shape=jax.ShapeDtypeStruct((n, d), F32),
        compiler_params=pltpu.CompilerParams(dimension_semantics=("arbitrary",), vmem_limit_bytes=VMEM_LIMIT),
        name="tail",
    )(x2d, ys, ya, *consts)


def kernel(x_prompt, x_sample, cache_k, cache_v, cache_kidx, state_ssm_re, state_ssm_im, page_table, norm_mix, w_in,
           lam_re, lam_im, log_step, b_re, b_im, c_re, c_im, d_skip, w_glu, b_glu, w_branch_ssm, w_branch_attn,
           w_out, norm_mlp, w_up, w_down, norm_final):
    depth, d, _ = w_in.shape
    assert depth == 1, "single-layer step"
    l = 0
    bp, tp, _ = x_prompt.shape
    bd, ts, _ = x_sample.shape
    _, n_phys, page, n_heads, head_dim = cache_k.shape
    idim = cache_kidx.shape[3]
    g, p = lam_re.shape[1:]
    hg = b_re.shape[3]
    sw, aw, iw, xw = g * hg, n_heads * head_dim, IDX_HEADS * idim, g * p
    n_pages = page_table.shape[1]
    past_len = n_pages * page

    o_g = sw + 3 * aw + iw + idim + IDX_HEADS
    w_main = jnp.pad(w_in[l][:, :o_g], ((0, 0), (0, LANE - idim - IDX_HEADS))).astype(BF16)
    gmix = norm_mix[l].reshape(1, d)
    tw = dict(gmix=gmix, wg=w_in[l][:, o_g:].astype(BF16), wbs=w_branch_ssm[l].astype(BF16),
              wba=w_branch_attn[l].astype(BF16), wo=w_out[l].astype(BF16), gmlp=norm_mlp[l].reshape(1, d),
              wup=w_up[l].astype(BF16), wdn=w_down[l].astype(BF16), gfin=norm_final.reshape(1, d))
    sp = _s5_setup(lam_re[l], lam_im[l], log_step[l], b_re[l], b_im[l], c_re[l], c_im[l], d_skip[l],
                   w_glu[l], b_glu[l])
    proj = functools.partial(_inproj, g=gmix, w_main=w_main, sw=sw, aw=aw, iw=iw,
                             q_scale=head_dim ** -0.5 * LOG2E, qi_scale=idim ** -0.5)

    def time_major(a, b, t):
        return jnp.swapaxes(a.reshape(b, t, -1), 0, 1).reshape(t * b, -1)

    def batch_major(a, b, t):
        return jnp.swapaxes(a.reshape(t, b, -1), 0, 1).reshape(b * t, -1)

    xp = x_prompt.reshape(bp * tp, d)
    u, q, k32, kb, v32, vb, qi, kiw = proj(xp, tm=512)
    ki32 = kiw[:, :idim]
    wi = kiw[:, idim:idim + IDX_HEADS]
    nbp = max(bp, SUBLANE)
    zero_state = jnp.zeros((nbp, xw), F32)
    y_tm, sre_p, sim_p = _s5_branch(time_major(u, bp, tp), zero_state, zero_state, bp, sp, rows_per_step=512)
    y_ssm = batch_major(y_tm, bp, tp)
    y_att = _prompt_attention(q.reshape(bp, tp, aw), kb.reshape(bp, tp, aw), vb.reshape(bp, tp, aw),
                              qi.reshape(bp, tp, iw), ki32.astype(BF16).reshape(bp, tp, idim),
                              wi.reshape(bp, tp, IDX_HEADS), n_heads=n_heads, head_dim=head_dim, blk=256)
    y_prompt = _tail(xp, y_ssm, y_att.reshape(bp * tp, aw), tw, tm=256).reshape(bp, tp, d)
    k_prompt = k32.reshape(1, bp, tp, n_heads, head_dim)
    v_prompt = v32.reshape(1, bp, tp, n_heads, head_dim)
    kidx_prompt = ki32.reshape(1, bp, tp, idim)
    ssm_re_prompt = sre_p[:bp].reshape(1, bp, g, p)
    ssm_im_prompt = sim_p[:bp].reshape(1, bp, g, p)

    xs = x_sample.reshape(bd * ts, d)
    u, q, k32, kb, v32, vb, qi, kiw = proj(xs, tm=512)
    ki32 = kiw[:, :idim]
    wi = kiw[:, idim:idim + IDX_HEADS]
    y_tm, sre_s, sim_s = _s5_branch(time_major(u, bd, ts), state_ssm_re[l].reshape(bd, xw),
                                    state_ssm_im[l].reshape(bd, xw), bd, sp, rows_per_step=bd * ts)
    y_ssm = batch_major(y_tm, bd, ts)
    n_top = min(TOP_K, (past_len + ts) // 4)
    qi_ht = jnp.swapaxes(qi.reshape(bd, ts, IDX_HEADS, idim), 1, 2).reshape(bd, IDX_HEADS * ts, idim)
    new_tile_t = lambda a: jnp.swapaxes(jnp.pad(a.reshape(bd, ts, -1), ((0, 0), (0, page - ts), (0, 0))), 1, 2)
    cache_kidx_t = jnp.swapaxes(cache_kidx[l:l + 1], 2, 3)
    cache_kt = jnp.transpose(cache_k[l:l + 1], (0, 1, 3, 4, 2)).reshape(1, n_phys, aw, page)
    cache_vt = jnp.transpose(cache_v[l:l + 1], (0, 1, 3, 4, 2)).reshape(1, n_phys, aw, page)
    madd = _sample_select(page_table, qi_ht, wi.reshape(bd, ts, IDX_HEADS), new_tile_t(ki32.astype(BF16)),
                          cache_kidx_t, n_top=n_top)
    qbd = jnp.einsum("bthd,hg->bhtgd", q.reshape(bd, ts, n_heads, head_dim),
                     jnp.eye(n_heads, dtype=BF16)).reshape(bd, n_heads * ts, aw)
    y_att = _sample_attend(page_table, qbd, madd, new_tile_t(kb), new_tile_t(vb), cache_kt, cache_vt,
                           n_heads=n_heads, head_dim=head_dim, pps=16)
    y_sample = _tail(xs, y_ssm, y_att.reshape(bd * ts, aw), tw, tm=256).reshape(bd, ts, d)
    k_sample = k32.reshape(1, bd, ts, n_heads, head_dim)
    v_sample = v32.reshape(1, bd, ts, n_heads, head_dim)
    kidx_sample = ki32.reshape(1, bd, ts, idim)
    ssm_re_sample = sre_s.reshape(1, bd, g, p)
    ssm_im_sample = sim_s.reshape(1, bd, g, p)

    return (y_prompt, y_sample, k_prompt, v_prompt, kidx_prompt, ssm_re_prompt, ssm_im_prompt,
            k_sample, v_sample, kidx_sample, ssm_re_sample, ssm_im_sample)
```

```python
import functools

import jax
import jax.numpy as jnp
from jax import lax
from jax.experimental import pallas as pl
from jax.experimental.pallas import tpu as pltpu

F32 = jnp.float32
BF16 = jnp.bfloat16
I32 = jnp.int32
I16 = jnp.int16

EPS = 1e-6
TOP_K = 256
IDX_HEADS = 4
INT_MIN = -(2 ** 31)
I16_MIN = -(2 ** 15)
KEY_NEG_INF = 0x807FFFFF - 2 ** 32
FLT_MAX = 3.4028234663852886e38
MAX_SETTLE_STEPS = 64
LOG2E = 1.4426950408889634
NEG = -1e30
LANE = 128
SUBLANE = 8
PACK16 = 16
VMEM_LIMIT = 56 * 1024 * 1024


def _const_spec(shape):
    nd = len(shape)
    return pl.BlockSpec(shape, lambda *_: (0,) * nd, pipeline_mode=pl.Buffered(1))


def _rms(x, g):
    return x * lax.rsqrt(jnp.mean(x * x, axis=-1, keepdims=True) + EPS) * g


def _inproj_body(x_ref, g_ref, w_ref, u_ref, q_ref, k_ref, kb_ref, v_ref, vb_ref, qi_ref, kiw_ref,
                 *, sw, aw, iw, q_scale, qi_scale):
    xn = _rms(x_ref[...], g_ref[...]).astype(BF16)
    p = jnp.dot(xn, w_ref[...], preferred_element_type=F32)
    o = 0
    u_ref[...] = p[:, o:o + sw]
    o += sw
    q_ref[...] = (p[:, o:o + aw] * q_scale).astype(BF16)
    o += aw
    k = p[:, o:o + aw]
    k_ref[...] = k
    kb_ref[...] = k.astype(BF16)
    o += aw
    v = p[:, o:o + aw]
    v_ref[...] = v
    vb_ref[...] = v.astype(BF16)
    o += aw
    qi_ref[...] = (p[:, o:o + iw] * qi_scale).astype(BF16)
    o += iw
    kiw_ref[...] = p[:, o:o + LANE]


def _inproj(x2d, g, w_main, *, sw, aw, iw, q_scale, qi_scale, tm):
    n, d = x2d.shape
    wtot = w_main.shape[1]
    row = lambda width: pl.BlockSpec((tm, width), lambda i: (i, 0))
    return pl.pallas_call(
        functools.partial(_inproj_body, sw=sw, aw=aw, iw=iw, q_scale=q_scale, qi_scale=qi_scale),
        grid=(n // tm,),
        in_specs=[row(d), _const_spec((1, d)), _const_spec((d, wtot))],
        out_specs=[row(sw), row(aw), row(aw), row(aw), row(aw), row(aw), row(iw), row(LANE)],
        out_shape=[
            jax.ShapeDtypeStruct((n, sw), F32),
            jax.ShapeDtypeStruct((n, aw), BF16),
            jax.ShapeDtypeStruct((n, aw), F32),
            jax.ShapeDtypeStruct((n, aw), BF16),
            jax.ShapeDtypeStruct((n, aw), F32),
            jax.ShapeDtypeStruct((n, aw), BF16),
            jax.ShapeDtypeStruct((n, iw), BF16),
            jax.ShapeDtypeStruct((n, LANE), F32),
        ],
        compiler_params=pltpu.CompilerParams(dimension_semantics=("arbitrary",), vmem_limit_bytes=VMEM_LIMIT),
        name="inproj",
    )(x2d, g, w_main)


def _s5_param_body(lre_ref, lim_ref, ls_ref, bre_ref, bim_ref, are_ref, aim_ref, bbre_ref, bbim_ref):
    lam_re = lre_ref[...]
    lam_im = lim_ref[...]
    dt = jnp.exp(ls_ref[...])
    mag = jnp.exp(lam_re * dt)
    ang = lam_im * dt
    ab_re = mag * jnp.cos(ang)
    ab_im = mag * jnp.sin(ang)
    nr = ab_re - 1.0
    ni = ab_im
    den = lam_re * lam_re + lam_im * lam_im
    z_re = (nr * lam_re + ni * lam_im) / den
    z_im = (ni * lam_re - nr * lam_im) / den
    are_ref[...] = ab_re
    aim_ref[...] = ab_im
    b_re = bre_ref[...]
    b_im = bim_ref[...]
    bbre_ref[...] = z_re[:, None, :] * b_re - z_im[:, None, :] * b_im
    bbim_ref[...] = z_re[:, None, :] * b_im + z_im[:, None, :] * b_re


def _s5_params(lam_re, lam_im, log_step, b_re, b_im):
    g, p = lam_re.shape
    hg = b_re.shape[2]
    bt_re = jnp.swapaxes(b_re, 1, 2)
    bt_im = jnp.swapaxes(b_im, 1, 2)
    return pl.pallas_call(
        _s5_param_body,
        out_shape=[jax.ShapeDtypeStruct((g, p), F32), jax.ShapeDtypeStruct((g, p), F32),
                   jax.ShapeDtypeStruct((g, hg, p), F32), jax.ShapeDtypeStruct((g, hg, p), F32)],
        name="s5_params",
    )(lam_re, lam_im, log_step.reshape(g, 1), bt_re, bt_im)


def _s5_body(u_ref, s0re_ref, s0im_ref, are_ref, aim_ref, bre_ref, bim_ref, cre_ref, cimn_ref, d_ref,
             wglu_ref, bglu_ref, y_ref, sre_ref, sim_ref, xre, xim, car_re, car_im, *, nb, kt_n, lane_chunk):
    step = pl.program_id(0)
    rows, sw = u_ref.shape
    xw = xre.shape[1]
    kw = sw // kt_n
    xk = xw // kt_n

    @pl.when(step == 0)
    def _():
        car_re[...] = s0re_ref[...]
        car_im[...] = s0im_ref[...]

    u = u_ref[...]
    ub = u.astype(BF16)
    for kt in range(kt_n):
        xre[:, kt * xk:(kt + 1) * xk] = jnp.dot(ub[:, kt * kw:(kt + 1) * kw], bre_ref[kt],
                                                preferred_element_type=F32)
        xim[:, kt * xk:(kt + 1) * xk] = jnp.dot(ub[:, kt * kw:(kt + 1) * kw], bim_ref[kt],
                                                preferred_element_type=F32)

    if nb < SUBLANE:
        assert 2 * nb == SUBLANE
        lo = lax.broadcasted_iota(I32, (SUBLANE, lane_chunk), 0) < nb
        for c in range(xw // lane_chunk):
            cs = slice(c * lane_chunk, (c + 1) * lane_chunk)
            a_r = jnp.broadcast_to(are_ref[:, cs], (SUBLANE, lane_chunk))
            a_i = jnp.broadcast_to(aim_ref[:, cs], (SUBLANE, lane_chunk))

            def tile_step(j, carry, cs=cs, a_r=a_r, a_i=a_i):
                cr, ci = carry
                r0 = pl.multiple_of(j * SUBLANE, SUBLANE)
                tr = xre[pl.ds(r0, SUBLANE), cs]
                ti = xim[pl.ds(r0, SUBLANE), cs]
                x1r = a_r * cr - a_i * ci + tr
                x1i = a_r * ci + a_i * cr + ti
                sr = pltpu.roll(x1r, nb, 0)
                si = pltpu.roll(x1i, nb, 0)
                x2r = a_r * sr - a_i * si + tr
                x2i = a_r * si + a_i * sr + ti
                xre[pl.ds(r0, SUBLANE), cs] = jnp.where(lo, x1r, x2r)
                xim[pl.ds(r0, SUBLANE), cs] = jnp.where(lo, x1i, x2i)
                return pltpu.roll(x2r, nb, 0), pltpu.roll(x2i, nb, 0)

            cr, ci = lax.fori_loop(0, rows // SUBLANE, tile_step, (car_re[:, cs], car_im[:, cs]))
            car_re[:, cs] = cr
            car_im[:, cs] = ci
    else:
        a_r = are_ref[...]
        a_i = aim_ref[...]
        for t in range(rows // nb):
            rs = slice(t * nb, (t + 1) * nb)
            if t == 0:
                pr, pi = car_re[...], car_im[...]
            else:
                ps = slice((t - 1) * nb, t * nb)
                pr, pi = xre[ps, :], xim[ps, :]
            xre[rs, :] = a_r * pr - a_i * pi + xre[rs, :]
            xim[rs, :] = a_r * pi + a_i * pr + xim[rs, :]
        ls = slice(rows - nb, rows)
        car_re[...] = xre[ls, :]
        car_im[...] = xim[ls, :]

    sre_ref[...] = car_re[...]
    sim_ref[...] = car_im[...]

    ys = []
    for kt in range(kt_n):
        xs = slice(kt * xk, (kt + 1) * xk)
        ys.append(jnp.dot(xre[:, xs].astype(BF16), cre_ref[kt], preferred_element_type=F32)
                  + jnp.dot(xim[:, xs].astype(BF16), cimn_ref[kt], preferred_element_type=F32))
    y = jnp.concatenate(ys, axis=1) + d_ref[...] * u
    y = jax.nn.gelu(y)
    gate = jnp.dot(y.astype(BF16), wglu_ref[...], preferred_element_type=F32) + bglu_ref[...]
    y_ref[...] = (y * jax.nn.sigmoid(gate)).astype(BF16)


def _s5_branch(u_tm, s0_re, s0_im, nb, sp, rows_per_step):
    n, sw = u_tm.shape
    nbp, xw = s0_re.shape
    kt_n = sp["bre"].shape[0]
    r = rows_per_step
    body = functools.partial(_s5_body, nb=nb, kt_n=kt_n, lane_chunk=4 * LANE)
    return pl.pallas_call(
        body,
        grid=(n // r,),
        in_specs=[pl.BlockSpec((r, sw), lambda i: (i, 0)),
                  _const_spec((nbp, xw)), _const_spec((nbp, xw)),
                  _const_spec((1, xw)), _const_spec((1, xw)),
                  _const_spec(sp["bre"].shape), _const_spec(sp["bim"].shape),
                  _const_spec(sp["cre"].shape), _const_spec(sp["cimn"].shape),
                  _const_spec((1, sw)), _const_spec((sw, sw)), _const_spec((1, sw))],
        out_specs=[pl.BlockSpec((r, sw), lambda i: (i, 0)),
                   pl.BlockSpec((nbp, xw), lambda i: (0, 0)), pl.BlockSpec((nbp, xw), lambda i: (0, 0))],
        out_shape=[jax.ShapeDtypeStruct((n, sw), BF16),
                   jax.ShapeDtypeStruct((nbp, xw), F32), jax.ShapeDtypeStruct((nbp, xw), F32)],
        scratch_shapes=[pltpu.VMEM((r, xw), F32), pltpu.VMEM((r, xw), F32),
                        pltpu.VMEM((nbp, xw), F32), pltpu.VMEM((nbp, xw), F32)],
        compiler_params=pltpu.CompilerParams(dimension_semantics=("arbitrary",), vmem_limit_bytes=VMEM_LIMIT),
        name="s5_branch",
    )(u_tm, s0_re, s0_im, sp["a_re"], sp["a_im"], sp["bre"], sp["bim"], sp["cre"], sp["cimn"],
      sp["d"], sp["wglu"], sp["bglu"])


def _s5_setup(lam_re, lam_im, log_step, b_re, b_im, c_re, c_im, d_skip, w_glu, b_glu, kt_n=2):
    g, p = lam_re.shape
    hg = b_re.shape[2]
    ab_re, ab_im, bbt_re, bbt_im = _s5_params(lam_re, lam_im, log_step, b_re, b_im)
    gl = g // kt_n
    eye = jnp.eye(gl, dtype=F32)

    def bmat(bbt):
        return jnp.einsum("kghp,gf->kghfp", bbt.reshape(kt_n, gl, hg, p), eye).reshape(
            kt_n, gl * hg, gl * p).astype(BF16)

    def cmat(c):
        return jnp.einsum("kghp,gf->kfpgh", c.reshape(kt_n, gl, hg, p), eye).reshape(
            kt_n, gl * p, gl * hg).astype(BF16)

    return dict(a_re=ab_re.reshape(1, g * p), a_im=ab_im.reshape(1, g * p),
                bre=bmat(bbt_re), bim=bmat(bbt_im), cre=cmat(c_re), cimn=cmat(-c_im),
                d=d_skip.reshape(1, g * hg), wglu=w_glu.astype(BF16), bglu=b_glu.reshape(1, -1))


def _sortable_key(score):
    bits = lax.bitcast_convert_type(score, I32)
    return bits ^ ((bits >> 31) & 0x7FFFFFFF)


def _key_to_score(key):
    key = jnp.maximum(key, KEY_NEG_INF)
    return lax.bitcast_convert_type(key ^ ((key >> 31) & 0x7FFFFFFF), F32)


def _settle_threshold(v, n_top, count_gt_ge, neighbours):
    def flag(c_gt, c_ge):
        return jnp.max(jnp.where((c_gt >= n_top) | (c_ge < n_top), 1, 0))

    def body(st):
        it, v, c_gt, c_ge, _ = st
        up, dn = neighbours(v)
        v = jnp.where(c_gt >= n_top, up, jnp.where(c_ge < n_top, dn, v))
        c_gt, c_ge = count_gt_ge(v)
        return it + 1, v, c_gt, c_ge, flag(c_gt, c_ge)

    c_gt, c_ge = count_gt_ge(v)
    st = lax.while_loop(lambda st: (st[0] < MAX_SETTLE_STEPS) & (st[4] > 0), body,
                        (jnp.int32(0), v, c_gt, c_ge, flag(c_gt, c_ge)))
    return st[1], st[2]


def _prompt_attn_body(qit_ref, wit_ref, qpt_ref, ki_ref, k_ref, vt_ref, lt_ref, o_ref,
                      sc_s, hi_s, lo_s, lg_s, p_s, m_s, alpha_s, acc_s, *, n_top, n_heads, head_dim, v_rows):
    i = pl.program_id(1)
    tk = ki_ref.shape[2]
    idim = ki_ref.shape[3]
    tq = qit_ref.shape[2]
    nkb = i + 1
    sub_iota = lax.broadcasted_iota(I32, (tk, tq), 0)
    q_pos = i * tq + lax.broadcasted_iota(I32, (tk, tq), 1)

    def score_block(j, _):
        kib = ki_ref[0, j]
        sc = None
        for h in range(IDX_HEADS):
            s = jnp.dot(kib, qit_ref[0, h * idim:(h + 1) * idim, :], preferred_element_type=F32)
            r = jnp.maximum(s, 0.0) * wit_ref[0, h:h + 1, :]
            sc = r if sc is None else sc + r
        sc = sc * (IDX_HEADS ** -0.5)
        allowed = sub_iota + j * tk <= q_pos
        sc_s[j] = jnp.where(allowed, sc, -jnp.inf)
        key = jnp.where(allowed, _sortable_key(sc), INT_MIN)
        hi_s[j] = (key >> 16).astype(I16)
        lo_s[j] = ((key & 0xFFFF) + I16_MIN).astype(I16)
        return 0

    lax.fori_loop(0, nkb, score_block, 0)

    def count16(ref, pred):
        def body(j, acc):
            hit = jnp.where(pred(ref[j]), jnp.int16(1), jnp.int16(0))
            parts = [hit[r * PACK16:(r + 1) * PACK16, :] for r in range(tk // PACK16)]
            while len(parts) > 1:
                parts = [parts[a] + parts[a + 1] for a in range(0, len(parts), 2)]
            return acc + parts[0]
        acc = lax.fori_loop(0, nkb, body, jnp.zeros((PACK16, tq), I16))
        return jnp.sum(acc.astype(I32), axis=0, keepdims=True)

    def select16(ref, want):
        thr = jnp.where(count16(ref, lambda kb: kb >= jnp.int16(0)) >= want, 0, I16_MIN).astype(I32)

        def bit_step(b, thr):
            cand = thr | lax.shift_left(jnp.int32(1), 14 - b)
            c16 = cand.astype(I16)
            return jnp.where(count16(ref, lambda kb: kb >= c16) >= want, cand, thr)

        return lax.fori_loop(0, 15, bit_step, thr)

    th = select16(hi_s, n_top)
    th16 = th.astype(I16)
    want_lo = n_top - count16(hi_s, lambda kb: kb > th16)

    def mask_lo(j, _):
        lo_s[j] = jnp.where(hi_s[j] == th16, lo_s[j], jnp.int16(I16_MIN))
        return 0

    lax.fori_loop(0, nkb, mask_lo, 0)
    tl = select16(lo_s, want_lo)
    thr_score = _key_to_score(lax.shift_left(th, 16) | (tl - I16_MIN))

    def count_gt_ge(v):
        def body(j, acc):
            s = sc_s[j]
            gt = jnp.where(s > v, 1, 0).reshape(tk // SUBLANE, SUBLANE, tq).sum(axis=0)
            ge = jnp.where(s >= v, 1, 0).reshape(tk // SUBLANE, SUBLANE, tq).sum(axis=0)
            return acc[0] + gt, acc[1] + ge
        zero = jnp.zeros((SUBLANE, tq), I32)
        gt, ge = lax.fori_loop(0, nkb, body, (zero, zero))
        return gt.sum(axis=0, keepdims=True), ge.sum(axis=0, keepdims=True)

    def neighbours(v):
        def body(j, acc):
            s = sc_s[j]
            up = jnp.where(s > v, s, jnp.inf).reshape(tk // SUBLANE, SUBLANE, tq).min(axis=0)
            dn = jnp.where(s < v, s, -jnp.inf).reshape(tk // SUBLANE, SUBLANE, tq).max(axis=0)
            return jnp.minimum(acc[0], up), jnp.maximum(acc[1], dn)
        up, dn = lax.fori_loop(0, nkb, body, (jnp.full((SUBLANE, tq), jnp.inf, F32),
                                               jnp.full((SUBLANE, tq), -jnp.inf, F32)))
        return up.min(axis=0, keepdims=True), dn.max(axis=0, keepdims=True)

    thr_score, n_gt = _settle_threshold(thr_score, n_top, count_gt_ge, neighbours)
    need = (n_top - n_gt).astype(F32)
    thr_score = jnp.maximum(thr_score, -FLT_MAX)

    m_s[...] = jnp.full(m_s.shape, NEG, F32)
    acc_s[...] = jnp.zeros(acc_s.shape, F32)
    pair = 2 * head_dim

    def attend_block(j, eq_seen):
        s = sc_s[j]
        eq = s == thr_score
        eqf = jnp.where(eq, 1.0, 0.0)
        rank = jnp.dot(lt_ref[...], eqf.astype(BF16), preferred_element_type=F32) + eq_seen
        sel = (s > thr_score) | (eq & (rank < need))
        madd = jnp.where(sel, 0.0, NEG)
        kblk = k_ref[0, j]
        vblk = vt_ref[0, j]

        def logits(h):
            c0 = (h // 2) * pair
            lg_s[h] = jnp.dot(kblk[:, c0:c0 + pair], qpt_ref[0, h * pair:(h + 1) * pair, :],
                              preferred_element_type=F32) + madd

        def softmax(h):
            m_old = m_s[h:h + 1, :]
            m_new = jnp.maximum(m_old, jnp.max(lg_s[h], axis=0, keepdims=True))
            alpha_s[h:h + 1, :] = jnp.exp2(m_old - m_new)
            m_s[h:h + 1, :] = m_new
            p_s[h] = jnp.exp2(lg_s[h] - m_new).astype(BF16)

        def apply_pv(h):
            vs = slice(h * v_rows, (h + 1) * v_rows)
            acc_s[vs, :] = alpha_s[h:h + 1, :] * acc_s[vs, :] + jnp.dot(vblk[vs, :], p_s[h],
                                                                        preferred_element_type=F32)

        for stage in (logits, softmax, apply_pv):
            for h in range(n_heads):
                stage(h)
        return eq_seen + jnp.sum(eqf, axis=0, keepdims=True)

    lax.fori_loop(0, nkb, attend_block, jnp.zeros((1, tq), F32))
    for h in range(n_heads):
        r0 = h * v_rows
        o_ref[0, h * head_dim:(h + 1) * head_dim, :] = (
            acc_s[r0:r0 + head_dim, :] / acc_s[r0 + head_dim:r0 + head_dim + 1, :]).astype(o_ref.dtype)


def _prompt_attention(q, kb, vb, qi, ki, wi, *, n_heads, head_dim, blk):
    b, t, aw = q.shape
    idim = ki.shape[2]
    nblk = t // blk
    n_top = min(TOP_K, t // 4)
    pair = 2 * head_dim
    qit = jnp.swapaxes(qi, 1, 2)
    wit = jnp.pad(jnp.swapaxes(wi, 1, 2), ((0, 0), (0, SUBLANE - IDX_HEADS), (0, 0)))
    q4 = q.reshape(b, t, n_heads // 2, 2, head_dim)
    sel = jnp.eye(2, dtype=q.dtype)
    qp = jnp.einsum("btgrd,rs->btgrsd", q4, sel).reshape(b, t, n_heads * pair)
    qpt = jnp.swapaxes(qp, 1, 2)
    ki_b = ki.reshape(b, nblk, blk, idim)
    k_b = kb.reshape(b, nblk, blk, aw)
    v_rows = head_dim + PACK16
    vt5 = jnp.transpose(vb.reshape(b, nblk, blk, n_heads, head_dim), (0, 1, 3, 4, 2))
    vt_b = jnp.concatenate([vt5, jnp.ones((b, nblk, n_heads, 1, blk), BF16),
                            jnp.zeros((b, nblk, n_heads, PACK16 - 1, blk), BF16)], axis=3)
    vt_b = vt_b.reshape(b, nblk, n_heads * v_rows, blk)
    lt = jnp.tril(jnp.ones((blk, blk), F32), -1).astype(BF16)

    def whole(shape):
        return pl.BlockSpec((1,) + shape, lambda bi, i: (bi, 0, 0, 0), pipeline_mode=pl.Buffered(1))

    body = functools.partial(_prompt_attn_body, n_top=n_top, n_heads=n_heads, head_dim=head_dim, v_rows=v_rows)
    out_t = pl.pallas_call(
        body,
        grid=(b, nblk),
        in_specs=[pl.BlockSpec((1, qit.shape[1], blk), lambda bi, i: (bi, 0, i)),
                  pl.BlockSpec((1, SUBLANE, blk), lambda bi, i: (bi, 0, i)),
                  pl.BlockSpec((1, qpt.shape[1], blk), lambda bi, i: (bi, 0, i)),
                  whole((nblk, blk, idim)), whole((nblk, blk, aw)), whole((nblk, n_heads * v_rows, blk)),
                  _const_spec((blk, blk))],
        out_specs=pl.BlockSpec((1, aw, blk), lambda bi, i: (bi, 0, i)),
        out_shape=jax.ShapeDtypeStruct((b, aw, t), BF16),
        scratch_shapes=[pltpu.VMEM((nblk, blk, blk), F32),
                        pltpu.VMEM((nblk, blk, blk), I16), pltpu.VMEM((nblk, blk, blk), I16),
                        pltpu.VMEM((n_heads, blk, blk), F32), pltpu.VMEM((n_heads, blk, blk), BF16),
                        pltpu.VMEM((n_heads, blk), F32), pltpu.VMEM((n_heads, blk), F32),
                        pltpu.VMEM((n_heads * v_rows, blk), F32)],
        compiler_params=pltpu.CompilerParams(dimension_semantics=("arbitrary", "arbitrary"),
                                             vmem_limit_bytes=VMEM_LIMIT),
        name="prompt_attention",
    )(qit, wit, qpt, ki_b, k_b, vt_b, lt)
    return jnp.swapaxes(out_t, 1, 2)


def _sample_select_body(pt_ref, qi_ref, wi_ref, kin_ref, ut_ref, *rest, n_pages, n_top):
    page_refs = rest[:n_pages]
    madd_ref = rest[n_pages]
    keys_s, sc_s = rest[n_pages + 1:]
    tq = wi_ref.shape[1]
    page = page_refs[0].shape[3]
    qi = qi_ref[0]
    w = wi_ref[0]

    def score_tile(p, kpage_t_bf16, allowed):
        s = jnp.dot(qi, kpage_t_bf16, preferred_element_type=F32)
        sc = None
        for h in range(IDX_HEADS):
            r = jnp.maximum(s[h * tq:(h + 1) * tq, :], 0.0) * w[:, h:h + 1]
            sc = r if sc is None else sc + r
        sc = sc * (IDX_HEADS ** -0.5)
        key = _sortable_key(sc)
        if allowed is not None:
            sc = jnp.where(allowed, sc, -jnp.inf)
            key = jnp.where(allowed, key, INT_MIN)
        sc_s[p] = sc
        keys_s[p] = key

    for p in range(n_pages):
        score_tile(p, page_refs[p][0, 0].astype(BF16), None)
    t_iota = lax.broadcasted_iota(I32, (tq, page), 0)
    lane_iota = lax.broadcasted_iota(I32, (tq, page), 1)
    score_tile(n_pages, kin_ref[0], lane_iota <= t_iota)
    n_tiles = n_pages + 1

    def count(preds):
        accs = [jnp.zeros((tq, page), I32) for _ in preds]
        for p in range(n_tiles):
            kb = keys_s[p]
            accs = [a + jnp.where(pred(kb), 1, 0) for a, pred in zip(accs, preds)]
        return [jnp.sum(a, axis=1, keepdims=True) for a in accs]

    def two_bit_step(s, thr):
        b_hi = lax.shift_left(jnp.int32(1), 31 - 2 * s)
        b_lo = lax.shift_left(jnp.int32(1), 30 - 2 * s)
        c1, c2, c3 = thr + b_lo, thr + b_hi, thr + b_hi + b_lo
        n1, n2, n3 = count([lambda kb: kb >= c1, lambda kb: kb >= c2, lambda kb: kb >= c3])
        return jnp.where(n3 >= n_top, c3, jnp.where(n2 >= n_top, c2, jnp.where(n1 >= n_top, c1, thr)))

    thr = lax.fori_loop(0, 16, two_bit_step, jnp.full((tq, 1), INT_MIN, I32))

    def count_gt_ge(v):
        gt = jnp.zeros((tq, page), I32)
        ge = jnp.zeros((tq, page), I32)
        for p in range(n_tiles):
            s = sc_s[p]
            gt = gt + jnp.where(s > v, 1, 0)
            ge = ge + jnp.where(s >= v, 1, 0)
        return jnp.sum(gt, axis=1, keepdims=True), jnp.sum(ge, axis=1, keepdims=True)

    def neighbours(v):
        up = jnp.full((tq, page), jnp.inf, F32)
        dn = jnp.full((tq, page), -jnp.inf, F32)
        for p in range(n_tiles):
            s = sc_s[p]
            up = jnp.minimum(up, jnp.where(s > v, s, jnp.inf))
            dn = jnp.maximum(dn, jnp.where(s < v, s, -jnp.inf))
        return jnp.min(up, axis=1, keepdims=True), jnp.max(dn, axis=1, keepdims=True)

    thr_score, n_gt = _settle_threshold(_key_to_score(thr), n_top, count_gt_ge, neighbours)
    need = (n_top - n_gt).astype(F32)
    thr_score = jnp.maximum(thr_score, -FLT_MAX)

    eq_seen = jnp.zeros((tq, 1), F32)
    for p in range(n_tiles):
        s = sc_s[p]
        eq = s == thr_score
        eqf = jnp.where(eq, 1.0, 0.0)
        rank = jnp.dot(eqf.astype(BF16), ut_ref[...], preferred_element_type=F32) + eq_seen
        sel = (s > thr_score) | (eq & (rank < need))
        madd_ref[0, p] = jnp.where(sel, 0.0, NEG)
        eq_seen = eq_seen + jnp.sum(eqf, axis=1, keepdims=True)


def _sample_select(page_table, qi_ht, wi, ki_new_t, cache_kidx_t, *, n_top):
    bd, n_pages = page_table.shape
    _, n_phys, idim, page = cache_kidx_t.shape
    tq = wi.shape[1]
    ut = jnp.triu(jnp.ones((page, page), F32), 1).astype(BF16)

    def page_spec(p):
        return pl.BlockSpec((1, 1, idim, page), lambda b, pt, p=p: (0, pt[b, p], 0, 0))

    grid_spec = pltpu.PrefetchScalarGridSpec(
        num_scalar_prefetch=1,
        grid=(bd,),
        in_specs=[pl.BlockSpec((1, IDX_HEADS * tq, idim), lambda b, pt: (b, 0, 0)),
                  pl.BlockSpec((1, tq, IDX_HEADS), lambda b, pt: (b, 0, 0)),
                  pl.BlockSpec((1, idim, page), lambda b, pt: (b, 0, 0)),
                  pl.BlockSpec((page, page), lambda b, pt: (0, 0))]
                 + [page_spec(p) for p in range(n_pages)],
        out_specs=pl.BlockSpec((1, n_pages + 1, tq, page), lambda b, pt: (b, 0, 0, 0)),
        scratch_shapes=[pltpu.VMEM((n_pages + 1, tq, page), I32), pltpu.VMEM((n_pages + 1, tq, page), F32)],
    )
    return pl.pallas_call(
        functools.partial(_sample_select_body, n_pages=n_pages, n_top=n_top),
        grid_spec=grid_spec,
        out_shape=jax.ShapeDtypeStruct((bd, n_pages + 1, tq, page), F32),
        compiler_params=pltpu.CompilerParams(dimension_semantics=("arbitrary",), vmem_limit_bytes=VMEM_LIMIT),
        name="sample_select",
    )(page_table, qi_ht, wi, ki_new_t, ut, *([cache_kidx_t] * n_pages))


def _sample_attend_body(pt_ref, qbd_ref, madd_ref, maddn_ref, kn_ref, vn_ref, *rest, pps, n_heads, head_dim):
    k_refs = rest[:pps]
    v_refs = rest[pps:2 * pps]
    o_ref = rest[2 * pps]
    m_s, l_s, acc_s = rest[2 * pps + 1:]
    c = pl.program_id(1)
    n_chunks = pl.num_programs(1)
    tq = madd_ref.shape[2]
    qbd = qbd_ref[0]
    nt = (((1,), (1,)), ((), ()))

    @pl.when(c == 0)
    def _():
        m_s[...] = jnp.full(m_s.shape, NEG, F32)
        l_s[...] = jnp.zeros(l_s.shape, F32)
        acc_s[...] = jnp.zeros(acc_s.shape, F32)

    def update(kt_tiles, vt_tiles, madd_tiles):
        lgs = []
        for kt, mt in zip(kt_tiles, madd_tiles):
            lg = jnp.dot(qbd, kt, preferred_element_type=F32)
            lgs.append(lg + jnp.concatenate([mt] * n_heads, axis=0))
        mx = lgs[0]
        for lg in lgs[1:]:
            mx = jnp.maximum(mx, lg)
        m_old = m_s[...]
        m_new = jnp.maximum(m_old, jnp.max(mx, axis=1, keepdims=True))
        alpha = jnp.exp2(m_old - m_new)
        psum = None
        pv = None
        for lg, vt in zip(lgs, vt_tiles):
            p = jnp.exp2(lg - m_new)
            psum = p if psum is None else psum + p
            d = lax.dot_general(p.astype(BF16), vt, nt, preferred_element_type=F32)
            pv = d if pv is None else pv + d
        l_s[...] = alpha * l_s[...] + jnp.sum(psum, axis=1, keepdims=True)
        acc_s[...] = alpha * acc_s[...] + pv
        m_s[...] = m_new

    update([k_refs[p][0, 0].astype(BF16) for p in range(pps)],
           [v_refs[p][0, 0].astype(BF16) for p in range(pps)],
           [madd_ref[0, p] for p in range(pps)])

    @pl.when(c == n_chunks - 1)
    def _():
        update([kn_ref[0]], [vn_ref[0]], [maddn_ref[0, 0]])
        accn = acc_s[...] / l_s[...]
        lane_head = lax.broadcasted_iota(I32, (tq, n_heads * head_dim), 1) // head_dim
        out = jnp.zeros((tq, n_heads * head_dim), F32)
        for h in range(n_heads):
            out = jnp.where(lane_head == h, accn[h * tq:(h + 1) * tq, :], out)
        o_ref[0] = out.astype(o_ref.dtype)


def _sample_attend(page_table, qbd, madd, k_new_t, v_new_t, cache_kt, cache_vt, *, n_heads, head_dim, pps):
    bd, n_pages = page_table.shape
    _, n_phys, aw, page = cache_kt.shape
    tq = madd.shape[2]
    rows = qbd.shape[1]

    def page_spec(p):
        return pl.BlockSpec((1, 1, aw, page), lambda b, c, pt, p=p: (0, pt[b, c * pps + p], 0, 0))

    grid_spec = pltpu.PrefetchScalarGridSpec(
        num_scalar_prefetch=1,
        grid=(bd, n_pages // pps),
        in_specs=[pl.BlockSpec((1, rows, aw), lambda b, c, pt: (b, 0, 0)),
                  pl.BlockSpec((1, pps, tq, page), lambda b, c, pt: (b, c, 0, 0)),
                  pl.BlockSpec((1, 1, tq, page), lambda b, c, pt: (b, n_pages, 0, 0)),
                  pl.BlockSpec((1, aw, page), lambda b, c, pt: (b, 0, 0)),
                  pl.BlockSpec((1, aw, page), lambda b, c, pt: (b, 0, 0))]
                 + [page_spec(p) for p in range(pps)] * 2,
        out_specs=pl.BlockSpec((1, tq, aw), lambda b, c, pt: (b, 0, 0)),
        scratch_shapes=[pltpu.VMEM((rows, 1), F32), pltpu.VMEM((rows, 1), F32), pltpu.VMEM((rows, aw), F32)],
    )
    return pl.pallas_call(
        functools.partial(_sample_attend_body, pps=pps, n_heads=n_heads, head_dim=head_dim),
        grid_spec=grid_spec,
        out_shape=jax.ShapeDtypeStruct((bd, tq, aw), BF16),
        compiler_params=pltpu.CompilerParams(dimension_semantics=("arbitrary", "arbitrary"),
                                             vmem_limit_bytes=VMEM_LIMIT),
        name="sample_attend",
    )(page_table, qbd, madd, madd, k_new_t, v_new_t, *([cache_kt] * pps), *([cache_vt] * pps))


def _tail_body(x_ref, ys_ref, ya_ref, gmix_ref, wg_ref, wbs_ref, wba_ref, wo_ref, gmlp_ref, wup_ref, wdn_ref,
               gfin_ref, o_ref):
    x = x_ref[...]
    d = x.shape[1]
    xn = _rms(x, gmix_ref[...]).astype(BF16)
    gates = jnp.dot(xn, wg_ref[...], preferred_element_type=F32)
    a = jnp.dot(ys_ref[...], wbs_ref[...], preferred_element_type=F32)
    b = jnp.dot(ya_ref[...], wba_ref[...], preferred_element_type=F32)
    merged = jax.nn.sigmoid(gates[:, :d]) * a + jax.nn.sigmoid(gates[:, d:]) * b
    h = x + jnp.dot(merged.astype(BF16), wo_ref[...], preferred_element_type=F32)
    hn = _rms(h, gmlp_ref[...]).astype(BF16)
    up = jnp.dot(hn, wup_ref[...], preferred_element_type=F32)
    act = jnp.square(jnp.maximum(up, 0.0)).astype(BF16)
    h = h + jnp.dot(act, wdn_ref[...], preferred_element_type=F32)
    o_ref[...] = _rms(h, gfin_ref[...])


def _tail(x2d, ys, ya, tw, *, tm):
    n, d = x2d.shape
    sw = ys.shape[1]
    aw = ya.shape[1]
    row = lambda width: pl.BlockSpec((tm, width), lambda i: (i, 0))
    consts = [tw["gmix"], tw["wg"], tw["wbs"], tw["wba"], tw["wo"], tw["gmlp"], tw["wup"], tw["wdn"], tw["gfin"]]
    return pl.pallas_call(
        _tail_body,
        grid=(n // tm,),
        in_specs=[row(d), row(sw), row(aw)] + [_const_spec(c.shape) for c in consts],
        out_specs=row(d),
        out_shape=jax.ShapeDtypeStruct((n, d), F32),
        compiler_params=pltpu.CompilerParams(dimension_semantics=("arbitrary",), vmem_limit_bytes=VMEM_LIMIT),
        name="tail",
    )(x2d, ys, ya, *consts)


def kernel(x_prompt, x_sample, cache_k, cache_v, cache_kidx, state_ssm_re, state_ssm_im, page_table, norm_mix, w_in,
           lam_re, lam_im, log_step, b_re, b_im, c_re, c_im, d_skip, w_glu, b_glu, w_branch_ssm, w_branch_attn,
           w_out, norm_mlp, w_up, w_down, norm_final):
    depth, d, _ = w_in.shape
    assert depth == 1, "single-layer step"
    l = 0
    bp, tp, _ = x_prompt.shape
    bd, ts, _ = x_sample.shape
    _, n_phys, page, n_heads, head_dim = cache_k.shape
    idim = cache_kidx.shape[3]
    g, p = lam_re.shape[1:]
    hg = b_re.shape[3]
    sw, aw, iw, xw = g * hg, n_heads * head_dim, IDX_HEADS * idim, g * p
    n_pages = page_table.shape[1]
    past_len = n_pages * page

    o_g = sw + 3 * aw + iw + idim + IDX_HEADS
    w_main = jnp.pad(w_in[l][:, :o_g], ((0, 0), (0, LANE - idim - IDX_HEADS))).astype(BF16)
    gmix = norm_mix[l].reshape(1, d)
    tw = dict(gmix=gmix, wg=w_in[l][:, o_g:].astype(BF16), wbs=w_branch_ssm[l].astype(BF16),
              wba=w_branch_attn[l].astype(BF16), wo=w_out[l].astype(BF16), gmlp=norm_mlp[l].reshape(1, d),
              wup=w_up[l].astype(BF16), wdn=w_down[l].astype(BF16), gfin=norm_final.reshape(1, d))
    sp = _s5_setup(lam_re[l], lam_im[l], log_step[l], b_re[l], b_im[l], c_re[l], c_im[l], d_skip[l],
                   w_glu[l], b_glu[l])
    proj = functools.partial(_inproj, g=gmix, w_main=w_main, sw=sw, aw=aw, iw=iw,
                             q_scale=head_dim ** -0.5 * LOG2E, qi_scale=idim ** -0.5)

    def time_major(a, b, t):
        return jnp.swapaxes(a.reshape(b, t, -1), 0, 1).reshape(t * b, -1)

    def batch_major(a, b, t):
        return jnp.swapaxes(a.reshape(t, b, -1), 0, 1).reshape(b * t, -1)

    xp = x_prompt.reshape(bp * tp, d)
    u, q, k32, kb, v32, vb, qi, kiw = proj(xp, tm=512)
    ki32 = kiw[:, :idim]
    wi = kiw[:, idim:idim + IDX_HEADS]
    nbp = max(bp, SUBLANE)
    zero_state = jnp.zeros((nbp, xw), F32)
    y_tm, sre_p, sim_p = _s5_branch(time_major(u, bp, tp), zero_state, zero_state, bp, sp, rows_per_step=512)
    y_ssm = batch_major(y_tm, bp, tp)
    y_att = _prompt_attention(q.reshape(bp, tp, aw), kb.reshape(bp, tp, aw), vb.reshape(bp, tp, aw),
                              qi.reshape(bp, tp, iw), ki32.astype(BF16).reshape(bp, tp, idim),
                              wi.reshape(bp, tp, IDX_HEADS), n_heads=n_heads, head_dim=head_dim, blk=256)
    y_prompt = _tail(xp, y_ssm, y_att.reshape(bp * tp, aw), tw, tm=256).reshape(bp, tp, d)
    k_prompt = k32.reshape(1, bp, tp, n_heads, head_dim)
    v_prompt = v32.reshape(1, bp, tp, n_heads, head_dim)
    kidx_prompt = ki32.reshape(1, bp, tp, idim)
    ssm_re_prompt = sre_p[:bp].reshape(1, bp, g, p)
    ssm_im_prompt = sim_p[:bp].reshape(1, bp, g, p)

    xs = x_sample.reshape(bd * ts, d)
    u, q, k32, kb, v32, vb, qi, kiw = proj(xs, tm=512)
    ki32 = kiw[:, :idim]
    wi = kiw[:, idim:idim + IDX_HEADS]
    y_tm, sre_s, sim_s = _s5_branch(time_major(u, bd, ts), state_ssm_re[l].reshape(bd, xw),
                                    state_ssm_im[l].reshape(bd, xw), bd, sp, rows_per_step=bd * ts)
    y_ssm = batch_major(y_tm, bd, ts)
    n_top = min(TOP_K, (past_len + ts) // 4)
    qi_ht = jnp.swapaxes(qi.reshape(bd, ts, IDX_HEADS, idim), 1, 2).reshape(bd, IDX_HEADS * ts, idim)
    new_tile_t = lambda a: jnp.swapaxes(jnp.pad(a.reshape(bd, ts, -1), ((0, 0), (0, page - ts), (0, 0))), 1, 2)
    cache_kidx_t = jnp.swapaxes(cache_kidx[l:l + 1], 2, 3)
    cache_kt = jnp.transpose(cache_k[l:l + 1], (0, 1, 3, 4, 2)).reshape(1, n_phys, aw, page)
    cache_vt = jnp.transpose(cache_v[l:l + 1], (0, 1, 3, 4, 2)).reshape(1, n_phys, aw, page)
    madd = _sample_select(page_table, qi_ht, wi.reshape(bd, ts, IDX_HEADS), new_tile_t(ki32.astype(BF16)),
                          cache_kidx_t, n_top=n_top)
    qbd = jnp.einsum("bthd,hg->bhtgd", q.reshape(bd, ts, n_heads, head_dim),
                     jnp.eye(n_heads, dtype=BF16)).reshape(bd, n_heads * ts, aw)
    y_att = _sample_attend(page_table, qbd, madd, new_tile_t(kb), new_tile_t(vb), cache_kt, cache_vt,
                           n_heads=n_heads, head_dim=head_dim, pps=32)
    y_sample = _tail(xs, y_ssm, y_att.reshape(bd * ts, aw), tw, tm=256).reshape(bd, ts, d)
    k_sample = k32.reshape(1, bd, ts, n_heads, head_dim)
    v_sample = v32.reshape(1, bd, ts, n_heads, head_dim)
    kidx_sample = ki32.reshape(1, bd, ts, idim)
    ssm_re_sample = sre_s.reshape(1, bd, g, p)
    ssm_im_sample = sim_s.reshape(1, bd, g, p)

    return (y_prompt, y_sample, k_prompt, v_prompt, kidx_prompt, ssm_re_prompt, ssm_im_prompt,
            k_sample, v_sample, kidx_sample, ssm_re_sample, ssm_im_sample)
```
